```python
import math
import jax, jax.numpy as jnp
from jax import lax
import numpy as np

D_MODEL = 2048
BATCH = 4
SEQ = 2048
DEPTH = 2
DEC_BATCH = 128
DEC_SEQ = 4
PAST_LEN = 16384
PAGE_SIZE = 128

D_MIX = D_MODEL
GROUP_W = D_MIX // 4
DIFF_HEADS = 4
DIFF_DH = GROUP_W // (2 * DIFF_HEADS)
DIFF_DV = 2 * DIFF_DH
MLA_HEADS = 4
MLA_DV = GROUP_W // MLA_HEADS
MLA_D_NOPE = 128
MLA_D_ROPE = 32
MLA_Q_RANK = 384
MLA_KV_RANK = 128
ROPE_THETA = 10000.0
NSA_HEADS = 8
NSA_DH = GROUP_W // NSA_HEADS
NSA_BLOCK = 64
NSA_TOPK = 16
NSA_WINDOW = 512
FORCE_SCORE = 1.0e4
SSD_HEADDIM = 64
SSD_HEADS = GROUP_W // SSD_HEADDIM
SSD_STATE = 128
SSD_GROUPS = 2
SSD_CONV = 4
SSD_CHUNK = 128
SSD_CONV_CH = GROUP_W + 2 * SSD_GROUPS * SSD_STATE
D_FF = 5632
Q_BLOCK = 128
EPS = 1e-6
NEG_INF = -1e30
TINY = 1e-30
F32 = jnp.float32

IN_SIZES = (DIFF_HEADS * 2 * DIFF_DH, 2 * DIFF_DH, DIFF_DV,
            MLA_Q_RANK, MLA_KV_RANK, MLA_D_ROPE,
            NSA_HEADS * NSA_DH, 6 * NSA_DH, 3 * NSA_HEADS,
            GROUP_W, SSD_CONV_CH, SSD_HEADS)
N_IN = sum(IN_SIZES)

kernel_name = 'hymba_macaron_diff_mla_nsa_ssd_step'


def rms_norm(x, g):
    x32 = x.astype(F32)
    y = x32 * lax.rsqrt(jnp.mean(x32 * x32, axis=-1, keepdims=True) + EPS)
    return (y * g.astype(F32)).astype(x.dtype)


def swiglu(x, w1, w2):
    gate, up = jnp.split(x @ w1, 2, axis=-1)
    return (jax.nn.silu(gate) * up) @ w2


def masked_softmax(s, mask):
    s = jnp.where(mask, s.astype(F32), NEG_INF)
    e = jnp.exp(s - jnp.max(s, axis=-1, keepdims=True)) * mask
    return e / jnp.maximum(jnp.sum(e, axis=-1, keepdims=True), TINY)


def alibi_slopes(n):
    return 2.0 ** (-8.0 * jnp.arange(1, n + 1, dtype=F32) / n)


def rope(x, pos):
    half = x.shape[-1] // 2
    inv = ROPE_THETA ** (-jnp.arange(half, dtype=F32) / half)
    ang = pos.astype(F32)[:, None] * inv
    ang = ang.reshape(ang.shape[0], *([1] * (x.ndim - 3)), half)
    cos, sin = jnp.cos(ang), jnp.sin(ang)
    x1, x2 = x[..., :half].astype(F32), x[..., half:].astype(F32)
    return jnp.concatenate([x1 * cos - x2 * sin, x1 * sin + x2 * cos], axis=-1).astype(x.dtype)


def over_query_blocks(fn, q_pos, *q_args):
    T = q_pos.shape[0]
    blk = Q_BLOCK if T % Q_BLOCK == 0 else T
    nb = T // blk

    def split(a):
        return jnp.moveaxis(a.reshape(a.shape[0], nb, blk, *a.shape[2:]), 1, 0)

    out = lax.map(lambda args: fn(*args), (q_pos.reshape(nb, blk),) + tuple(split(a) for a in q_args))
    out = jnp.moveaxis(out, 0, 1)
    return out.reshape(out.shape[0], T, *out.shape[3:])


def gather_pages(pool, l, page_table):
    g = pool[l, page_table]
    return g.reshape(page_table.shape[0], -1, pool.shape[-1])


def diff_attention(q, k, v, q_pos, k_pos, lam, lam_init, subln_g):
    slopes = alibi_slopes(DIFF_HEADS)[:, None, None]

    def block(pb, qb):
        s = jnp.einsum('bqhcd,bkcd->bchqk', qb, k) * DIFF_DH ** -0.5
        dist = pb[:, None] - k_pos[None, :]
        s = s.astype(F32) - slopes * dist.astype(F32)
        p = masked_softmax(s, dist >= 0)
        w = p[:, 0] - lam * p[:, 1]
        return jnp.einsum('bhqk,bkd->bqhd', w, v)

    o = over_query_blocks(block, q_pos, q)
    o = rms_norm(o, subln_g) * (1.0 - lam_init)
    return o.reshape(o.shape[0], o.shape[1], -1)


def mla_attention(q_lat, q_rope, ckv, krope, q_pos, k_pos):
    scale = (MLA_D_NOPE + MLA_D_ROPE) ** -0.5

    def block(pb, qlb, qrb):
        s = (jnp.einsum('bqhr,bkr->bhqk', qlb, ckv) + jnp.einsum('bqhd,bkd->bhqk', qrb, krope)) * scale
        p = masked_softmax(s, k_pos[None, :] <= pb[:, None])
        return jnp.einsum('bhqk,bkr->bqhr', p, ckv)

    return over_query_blocks(block, q_pos, q_lat, q_rope)


def nsa_attention(q, gates, kc_seq, vc_seq, ks_seq, vs_seq, kw_seq, vw_seq, q_pos, w_start):
    B = q.shape[0]
    Tk = kc_seq.shape[1]
    nblk = -(-Tk // NSA_BLOCK)
    pad = nblk * NSA_BLOCK - Tk

    def blocks(a):
        return jnp.pad(a, ((0, 0), (0, pad), (0, 0))).reshape(B, nblk, NSA_BLOCK, NSA_DH)

    kc = blocks(kc_seq).mean(axis=2)
    vc = blocks(vc_seq).mean(axis=2)
    ks, vs = blocks(ks_seq), blocks(vs_seq)
    kw = jnp.pad(kw_seq, ((0, 0), (NSA_WINDOW, 0), (0, 0)))
    vw = jnp.pad(vw_seq, ((0, 0), (NSA_WINDOW, 0), (0, 0)))
    blk_ids = jnp.arange(nblk)
    blk_end = (blk_ids + 1) * NSA_BLOCK - 1
    offs = jnp.arange(NSA_BLOCK)
    n_sel = min(NSA_TOPK, nblk)
    slopes = alibi_slopes(NSA_HEADS)[:, None, None]
    scale = NSA_DH ** -0.5

    def block(pb, qb, gb):
        nq = pb.shape[0]
        dist_c = pb[:, None] - blk_end[None, :]
        s_c = jnp.einsum('bqhd,bnd->bhqn', qb, kc) * scale - slopes * dist_c
        p_c = masked_softmax(s_c, dist_c >= 0)
        o_c = jnp.einsum('bhqn,bnd->bqhd', p_c, vc)
        cur = pb // NSA_BLOCK
        imp = p_c.sum(axis=1)
        forced = (blk_ids[None] == 0) | (blk_ids[None] == cur[:, None]) | (blk_ids[None] == cur[:, None] - 1)
        valid = blk_ids[None] <= cur[:, None]
        score = jnp.where(valid, jnp.where(forced, FORCE_SCORE, imp), -1.0)
        top_val, sel = lax.top_k(score, n_sel)
        ks_sel = jax.vmap(lambda a, i: a[i])(ks, sel).reshape(B, nq, n_sel * NSA_BLOCK, NSA_DH)
        vs_sel = jax.vmap(lambda a, i: a[i])(vs, sel).reshape(B, nq, n_sel * NSA_BLOCK, NSA_DH)
        sel_pos = sel[..., None] * NSA_BLOCK + offs
        dist_s = (pb[None, :, None, None] - sel_pos).reshape(B, nq, n_sel * NSA_BLOCK)
        mask_s = ((dist_s >= 0) & jnp.repeat(top_val >= 0, NSA_BLOCK, axis=-1))[:, None]
        s_s = jnp.einsum('bqhd,bqkd->bhqk', qb, ks_sel) * scale - slopes * dist_s[:, None]
        p_s = masked_softmax(s_s, mask_s)
        o_s = jnp.einsum('bhqk,bqkd->bqhd', p_s, vs_sel)
        r0 = pb[0] - w_start
        kwb = lax.dynamic_slice_in_dim(kw, r0, NSA_WINDOW + nq, axis=1)
        vwb = lax.dynamic_slice_in_dim(vw, r0, NSA_WINDOW + nq, axis=1)
        w_pos = pb[0] - NSA_WINDOW + jnp.arange(NSA_WINDOW + nq)
        dist_w = pb[:, None] - w_pos[None, :]
        mask_w = (dist_w >= 0) & (dist_w < NSA_WINDOW) & (w_pos[None, :] >= w_start)
        s_w = jnp.einsum('bqhd,bkd->bhqk', qb, kwb) * scale - slopes * dist_w
        p_w = masked_softmax(s_w, mask_w)
        o_w = jnp.einsum('bhqk,bkd->bqhd', p_w, vwb)
        g = jax.nn.sigmoid(gb.astype(F32))
        return g[..., 0:1] * o_c + g[..., 1:2] * o_s + g[..., 2:3] * o_w

    return over_query_blocks(block, q_pos, q, gates)


def ssd_scan(x, dt, A, Bm, Cm, h0):
    B, T, H, P = x.shape
    Q = SSD_CHUNK if T % SSD_CHUNK == 0 else T
    nc = T // Q

    def chunks(a):
        return jnp.moveaxis(a.reshape(B, nc, Q, *a.shape[2:]), 1, 0)

    causal = jnp.tril(jnp.ones((Q, Q), bool))[None, :, :, None]

    def step(h, inp):
        xc, dtc, Bc, Cc = inp
        acum = jnp.cumsum(dtc * A, axis=1)
        seg = acum[:, :, None, :] - acum[:, None, :, :]
        decay = jnp.where(causal, jnp.exp(jnp.where(causal, seg, 0.0)), 0.0)
        w = jnp.einsum('bihn,bjhn->bijh', Cc, Bc) * decay * dtc[:, None, :, :]
        y = (jnp.einsum('bijh,bjhp->bihp', w, xc)
             + jnp.einsum('bihn,bhpn->bihp', Cc, h) * jnp.exp(acum)[..., None])
        to_end = jnp.exp(acum[:, -1:, :] - acum) * dtc
        h = h * jnp.exp(acum[:, -1])[:, :, None, None] + jnp.einsum('bjh,bjhn,bjhp->bhpn', to_end, Bc, xc)
        return h, y

    h, ys = lax.scan(step, h0, (chunks(x), chunks(dt), chunks(Bm), chunks(Cm)))
    return jnp.moveaxis(ys, 0, 1).reshape(B, T, H, P), h


def ssd_mixer(z, xbc, dt_raw, conv_state, h0, conv_w, conv_b, dt_bias, a_log, d_skip, norm_g):
    B, T, _ = xbc.shape
    xpad = jnp.concatenate([conv_state.astype(xbc.dtype), xbc], axis=1)
    new_conv = xpad[:, -(SSD_CONV - 1):]
    conv = lax.conv_general_dilated(xpad, conv_w[:, None, :].astype(xpad.dtype), window_strides=(1,),
                                    padding='VALID', dimension_numbers=('NWC', 'WIO', 'NWC'),
                                    feature_group_count=SSD_CONV_CH) + conv_b
    xbc = jax.nn.silu(conv).astype(F32)
    xs, Bm, Cm = jnp.split(xbc, [GROUP_W, GROUP_W + SSD_GROUPS * SSD_STATE], axis=-1)
    xs = xs.reshape(B, T, SSD_HEADS, SSD_HEADDIM)
    rep = SSD_HEADS // SSD_GROUPS
    Bm = jnp.repeat(Bm.reshape(B, T, SSD_GROUPS, SSD_STATE), rep, axis=2)
    Cm = jnp.repeat(Cm.reshape(B, T, SSD_GROUPS, SSD_STATE), rep, axis=2)
    dt = jax.nn.softplus(dt_raw.astype(F32) + dt_bias.astype(F32))
    A = -jnp.exp(a_log.astype(F32))
    y, h = ssd_scan(xs, dt, A, Bm, Cm, h0.astype(F32))
    y = y + d_skip.astype(F32)[:, None] * xs
    y = y.reshape(B, T, GROUP_W) * jax.nn.silu(z.astype(F32))
    return rms_norm(y, norm_g), new_conv, h


def trunk_layer(x, l, p, past, past_len, w_buf):
    B, T, _ = x.shape
    pos = past_len + jnp.arange(T)
    x = x + 0.5 * rms_norm(swiglu(rms_norm(x, p['ffn1_pre_g']), p['ffn1_w1'], p['ffn1_w2']), p['ffn1_post_g'])
    u = rms_norm(x, p['mix_pre_g'])
    offsets = np.cumsum(IN_SIZES)[:-1].tolist()
    (dq, dk, dv, mcq, mckv, mkr, nq, nkv, ngate, sz, sxbc, sdt) = jnp.split(u @ p['w_in'], offsets, axis=-1)

    new_diff = jnp.concatenate([dk, dv], axis=-1)
    new_mla = jnp.concatenate([rms_norm(mckv, p['mla_kv_norm_g']), rope(mkr, pos)], axis=-1)
    new_nsa = nkv[..., :4 * NSA_DH]
    new_win_rows = nkv[..., 4 * NSA_DH:]

    if past is None:
        full_diff, full_mla, full_nsa, win_seq = new_diff, new_mla, new_nsa, new_win_rows
        w_start = 0
        conv_state = jnp.zeros((B, SSD_CONV - 1, SSD_CONV_CH), x.dtype)
        h0 = jnp.zeros((B, SSD_HEADS, SSD_HEADDIM, SSD_STATE), F32)
    else:
        full_diff = jnp.concatenate([past['diff'], new_diff], axis=1)
        full_mla = jnp.concatenate([past['mla'], new_mla], axis=1)
        full_nsa = jnp.concatenate([past['nsa'], new_nsa], axis=1)
        win_seq = jnp.concatenate([past['win'], new_win_rows], axis=1)
        w_start = past_len - past['win'].shape[1]
        conv_state, h0 = past['conv'], past['h']
    Tk = full_diff.shape[1]
    k_pos = jnp.arange(Tk)
    new_win = jnp.pad(win_seq, ((0, 0), (max(0, w_buf - win_seq.shape[1]), 0), (0, 0)))[:, -w_buf:]

    lv = p['diff_lambda'].astype(F32)
    lam_init = 0.8 - 0.6 * math.exp(-0.3 * l)
    lam = jnp.exp(jnp.sum(lv[0] * lv[1])) - jnp.exp(jnp.sum(lv[2] * lv[3])) + lam_init
    o_a = diff_attention(dq.reshape(B, T, DIFF_HEADS, 2, DIFF_DH),
                         full_diff[..., :2 * DIFF_DH].reshape(B, Tk, 2, DIFF_DH),
                         full_diff[..., 2 * DIFF_DH:], pos, k_pos, lam, lam_init, p['diff_subln_g'])

    qm = (rms_norm(mcq, p['mla_q_norm_g']) @ p['mla_w_uq']).reshape(B, T, MLA_HEADS, MLA_D_NOPE + MLA_D_ROPE)
    q_nope, q_rope = qm[..., :MLA_D_NOPE], rope(qm[..., MLA_D_NOPE:], pos)
    q_lat = jnp.einsum('bqhn,rhn->bqhr', q_nope, p['mla_w_uk'])
    o_lat = mla_attention(q_lat, q_rope, full_mla[..., :MLA_KV_RANK], full_mla[..., MLA_KV_RANK:], pos, k_pos)
    o_b = jnp.einsum('bqhr,rhd->bqhd', o_lat, p['mla_w_uv']).reshape(B, T, GROUP_W)
    o_b = rms_norm(o_b, p['mla_out_g'])

    o_c = nsa_attention(nq.reshape(B, T, NSA_HEADS, NSA_DH), ngate.reshape(B, T, NSA_HEADS, 3),
                        full_nsa[..., :NSA_DH], full_nsa[..., NSA_DH:2 * NSA_DH],
                        full_nsa[..., 2 * NSA_DH:3 * NSA_DH], full_nsa[..., 3 * NSA_DH:],
                        win_seq[..., :NSA_DH], win_seq[..., NSA_DH:], pos, w_start)
    o_c = rms_norm(o_c.reshape(B, T, GROUP_W), p['nsa_out_g'])

    o_d, new_conv, new_h = ssd_mixer(sz, sxbc, sdt, conv_state, h0, p['ssd_conv_w'], p['ssd_conv_b'],
                                     p['ssd_dt_bias'], p['ssd_a_log'], p['ssd_d'], p['ssd_norm_g'])

    mix = jnp.concatenate([o.astype(x.dtype) for o in (o_a, o_b, o_c, o_d)], axis=-1) @ p['w_out']
    x = x + rms_norm(mix, p['mix_post_g'])
    x = x + 0.5 * rms_norm(swiglu(rms_norm(x, p['ffn2_pre_g']), p['ffn2_w1'], p['ffn2_w2']), p['ffn2_post_g'])
    return x, (new_diff, new_mla, new_nsa, new_win, new_conv, new_h)


def setup_inputs(seed: int = 0) -> dict:
    key = jax.random.key(seed)
    ks = iter(jax.random.split(key, 48))

    def nrm(shape, scale=1.0):
        return jax.random.normal(next(ks), shape, F32) * scale

    def gain(n):
        return 1.0 + nrm((DEPTH, n), 0.02)

    n_pages = PAST_LEN // PAGE_SIZE
    n_pool = (5 * DEC_BATCH * n_pages) // 4
    w_buf = min(NSA_WINDOW, PAST_LEN)
    page_table = jax.random.permutation(next(ks), n_pool)[:DEC_BATCH * n_pages].reshape(DEC_BATCH, n_pages).astype(jnp.int32)
    dt0 = jnp.exp(jax.random.uniform(next(ks), (DEPTH, SSD_HEADS), F32, math.log(1e-3), math.log(1e-1)))
    dt_bias = dt0 + jnp.log(-jnp.expm1(-dt0))
    a_log = jnp.log(jax.random.uniform(next(ks), (DEPTH, SSD_HEADS), F32, 1.0, 16.0))
    return {
        'x_prompt': nrm((BATCH, SEQ, D_MODEL)),
        'x_sample': nrm((DEC_BATCH, DEC_SEQ, D_MODEL)),
        'cache_diff_kv': nrm((DEPTH, n_pool, PAGE_SIZE, 2 * DIFF_DH + DIFF_DV)),
        'cache_mla': nrm((DEPTH, n_pool, PAGE_SIZE, MLA_KV_RANK + MLA_D_ROPE)),
        'cache_nsa_kv': nrm((DEPTH, n_pool, PAGE_SIZE, 4 * NSA_DH)),
        'cache_nsa_win': nrm((DEPTH, DEC_BATCH, w_buf, 2 * NSA_DH)),
        'state_ssd_conv': nrm((DEPTH, DEC_BATCH, SSD_CONV - 1, SSD_CONV_CH)),
        'state_ssd_h': nrm((DEPTH, DEC_BATCH, SSD_HEADS, SSD_HEADDIM, SSD_STATE), 0.1),
        'page_table': page_table,
        'ffn1_pre_g': gain(D_MODEL),
        'ffn1_post_g': gain(D_MODEL),
        'ffn1_w1': nrm((DEPTH, D_MODEL, 2 * D_FF), D_MODEL ** -0.5),
        'ffn1_w2': nrm((DEPTH, D_FF, D_MODEL), D_FF ** -0.5),
        'mix_pre_g': gain(D_MODEL),
        'mix_post_g': gain(D_MODEL),
        'w_in': nrm((DEPTH, D_MODEL, N_IN), D_MODEL ** -0.5),
        'w_out': nrm((DEPTH, D_MIX, D_MODEL), D_MIX ** -0.5),
        'diff_lambda': nrm((DEPTH, 4, DIFF_DH), 0.1),
        'diff_subln_g': gain(DIFF_DV),
        'mla_q_norm_g': gain(MLA_Q_RANK),
        'mla_w_uq': nrm((DEPTH, MLA_Q_RANK, MLA_HEADS * (MLA_D_NOPE + MLA_D_ROPE)), MLA_Q_RANK ** -0.5),
        'mla_kv_norm_g': gain(MLA_KV_RANK),
        'mla_w_uk': nrm((DEPTH, MLA_KV_RANK, MLA_HEADS, MLA_D_NOPE), MLA_KV_RANK ** -0.5),
        'mla_w_uv': nrm((DEPTH, MLA_KV_RANK, MLA_HEADS, MLA_DV), MLA_KV_RANK ** -0.5),
        'mla_out_g': gain(GROUP_W),
        'nsa_out_g': gain(GROUP_W),
        'ssd_conv_w': nrm((DEPTH, SSD_CONV, SSD_CONV_CH), SSD_CONV ** -0.5),
        'ssd_conv_b': nrm((DEPTH, SSD_CONV_CH), 0.02),
        'ssd_dt_bias': dt_bias,
        'ssd_a_log': a_log,
        'ssd_d': 1.0 + nrm((DEPTH, SSD_HEADS), 0.02),
        'ssd_norm_g': gain(GROUP_W),
        'ffn2_pre_g': gain(D_MODEL),
        'ffn2_post_g': gain(D_MODEL),
        'ffn2_w1': nrm((DEPTH, D_MODEL, 2 * D_FF), D_MODEL ** -0.5),
        'ffn2_w2': nrm((DEPTH, D_FF, D_MODEL), D_FF ** -0.5),
    }


def reference(x_prompt, x_sample, cache_diff_kv, cache_mla, cache_nsa_kv, cache_nsa_win, state_ssd_conv,
              state_ssd_h, page_table, ffn1_pre_g, ffn1_post_g, ffn1_w1, ffn1_w2, mix_pre_g, mix_post_g,
              w_in, w_out, diff_lambda, diff_subln_g, mla_q_norm_g, mla_w_uq, mla_kv_norm_g, mla_w_uk,
              mla_w_uv, mla_out_g, nsa_out_g, ssd_conv_w, ssd_conv_b, ssd_dt_bias, ssd_a_log, ssd_d,
              ssd_norm_g, ffn2_pre_g, ffn2_post_g, ffn2_w1, ffn2_w2):
    weights = dict(ffn1_pre_g=ffn1_pre_g, ffn1_post_g=ffn1_post_g, ffn1_w1=ffn1_w1, ffn1_w2=ffn1_w2,
                   mix_pre_g=mix_pre_g, mix_post_g=mix_post_g, w_in=w_in, w_out=w_out,
                   diff_lambda=diff_lambda, diff_subln_g=diff_subln_g, mla_q_norm_g=mla_q_norm_g,
                   mla_w_uq=mla_w_uq, mla_kv_norm_g=mla_kv_norm_g, mla_w_uk=mla_w_uk, mla_w_uv=mla_w_uv,
                   mla_out_g=mla_out_g, nsa_out_g=nsa_out_g, ssd_conv_w=ssd_conv_w, ssd_conv_b=ssd_conv_b,
                   ssd_dt_bias=ssd_dt_bias, ssd_a_log=ssd_a_log, ssd_d=ssd_d, ssd_norm_g=ssd_norm_g,
                   ffn2_pre_g=ffn2_pre_g, ffn2_post_g=ffn2_post_g, ffn2_w1=ffn2_w1, ffn2_w2=ffn2_w2)
    past_len = page_table.shape[1] * cache_diff_kv.shape[2]
    w_buf = cache_nsa_win.shape[2]
    y_p, y_s = x_prompt, x_sample
    st_p, st_s = [], []
    for l in range(DEPTH):
        p = {name: w[l] for name, w in weights.items()}
        y_p, new_p = trunk_layer(y_p, l, p, None, 0, w_buf)
        past = dict(diff=gather_pages(cache_diff_kv, l, page_table),
                    mla=gather_pages(cache_mla, l, page_table),
                    nsa=gather_pages(cache_nsa_kv, l, page_table),
                    win=cache_nsa_win[l], conv=state_ssd_conv[l], h=state_ssd_h[l])
        y_s, new_s = trunk_layer(y_s, l, p, past, past_len, w_buf)
        st_p.append(new_p)
        st_s.append(new_s)

    def stacked(states, i):
        return jnp.stack([s[i] for s in states])

    return (y_p, y_s,
            stacked(st_p, 0), stacked(st_s, 0),
            stacked(st_p, 1), stacked(st_s, 1),
            stacked(st_p, 2), stacked(st_s, 2),
            stacked(st_p, 3), stacked(st_s, 3),
            stacked(st_p, 4), stacked(st_s, 4),
            stacked(st_p, 5), stacked(st_s, 5))
```

```python
import functools
import math

import numpy as np
import jax
import jax.numpy as jnp
from jax import lax
from jax.experimental import pallas as pl
from jax.experimental.pallas import tpu as pltpu

D_MODEL = 2048
GROUP_W = 512
DIFF_HEADS = 4
DIFF_DH = 64
DIFF_DV = 128
MLA_HEADS = 4
MLA_DV = 128
MLA_D_NOPE = 128
MLA_D_ROPE = 32
MLA_Q_RANK = 384
MLA_KV_RANK = 128
ROPE_THETA = 10000.0
NSA_HEADS = 8
NSA_DH = 64
NSA_BLOCK = 64
NSA_TOPK = 16
NSA_WINDOW = 512
FORCE_SCORE = 1.0e4
SSD_HEADDIM = 64
SSD_HEADS = 8
SSD_STATE = 128
SSD_GROUPS = 2
SSD_CONV = 4
SSD_CHUNK = 128
SSD_CONV_CH = 1024
D_FF = 5632
Q_BLOCK = 128
EPS = 1e-6
NEG_INF = -1e30
TINY = 1e-30
F32 = jnp.float32
BF16 = jnp.bfloat16

COL_DQ = 0
COL_NQ = 512
COL_SZ = 1024
COL_DK = 1536
COL_DV = 1664
COL_MCKV = 1792
COL_SMALL = 1920
COL_SXBC = 2048
COL_MCQ = 3072
COL_NKV = 3456
N_IN_PAD = 3840

VMEM_LIMIT = 56 * 1024 * 1024


def _rms(x, g):
    return x * lax.rsqrt(jnp.mean(x * x, axis=-1, keepdims=True) + EPS) * g


def _ffn_body(x_ref, pre_ref, post_ref, wg_ref, wu_ref, w2_ref, o_ref, xn_ref, acc_ref):
    j = pl.program_id(1)

    @pl.when(j == 0)
    def _():
        xn_ref[...] = _rms(x_ref[...], pre_ref[...]).astype(BF16)
        acc_ref[...] = jnp.zeros_like(acc_ref)

    xn = xn_ref[...]
    g = jnp.dot(xn, wg_ref[...], preferred_element_type=F32)
    u = jnp.dot(xn, wu_ref[...], preferred_element_type=F32)
    h = (g * jax.nn.sigmoid(g) * u).astype(BF16)
    acc_ref[...] += jnp.dot(h, w2_ref[...], preferred_element_type=F32)

    @pl.when(j == pl.num_programs(1) - 1)
    def _():
        o_ref[...] = x_ref[...] + 0.5 * _rms(acc_ref[...], post_ref[...])


def _ffn(x, pre_g, post_g, w1, w2, *, tm=544, tf=512):
    m, d = x.shape
    nf = D_FF // tf
    return pl.pallas_call(
        _ffn_body,
        grid=(m // tm, nf),
        in_specs=[
            pl.BlockSpec((tm, d), lambda i, j: (i, 0)),
            pl.BlockSpec((1, d), lambda i, j: (0, 0)),
            pl.BlockSpec((1, d), lambda i, j: (0, 0)),
            pl.BlockSpec((d, tf), lambda i, j: (0, j)),
            pl.BlockSpec((d, tf), lambda i, j: (0, j + nf)),
            pl.BlockSpec((tf, d), lambda i, j: (j, 0)),
        ],
        out_specs=pl.BlockSpec((tm, d), lambda i, j: (i, 0)),
        out_shape=jax.ShapeDtypeStruct((m, d), F32),
        scratch_shapes=[pltpu.VMEM((tm, d), BF16), pltpu.VMEM((tm, d), F32)],
        compiler_params=pltpu.CompilerParams(
            dimension_semantics=("parallel", "arbitrary"), vmem_limit_bytes=VMEM_LIMIT),
        name="ffn",
    )(x, pre_g.reshape(1, d), post_g.reshape(1, d), w1, w1, w2)


def _in_proj_body(x_ref, g_ref, w_ref, o_ref, xn_ref):
    @pl.when(pl.program_id(1) == 0)
    def _():
        xn_ref[...] = _rms(x_ref[...], g_ref[...]).astype(BF16)

    o_ref[...] = jnp.dot(xn_ref[...], w_ref[...], preferred_element_type=F32)


def _in_proj(x, g, w, *, tm=544, tn=1280):
    m, d = x.shape
    n = w.shape[1]
    return pl.pallas_call(
        _in_proj_body,
        grid=(m // tm, n // tn),
        in_specs=[
            pl.BlockSpec((tm, d), lambda i, j: (i, 0)),
            pl.BlockSpec((1, d), lambda i, j: (0, 0)),
            pl.BlockSpec((d, tn), lambda i, j: (0, j)),
        ],
        out_specs=pl.BlockSpec((tm, tn), lambda i, j: (i, j)),
        out_shape=jax.ShapeDtypeStruct((m, n), F32),
        scratch_shapes=[pltpu.VMEM((tm, d), BF16)],
        compiler_params=pltpu.CompilerParams(
            dimension_semantics=("parallel", "arbitrary"), vmem_limit_bytes=VMEM_LIMIT),
        name="in_proj",
    )(x, g.reshape(1, d), w)


def _out_proj_body(x_ref, a_ref, b_ref, c_ref, d_ref, w_ref, g_ref, o_ref):
    mix = jnp.dot(a_ref[...].astype(BF16), w_ref[0], preferred_element_type=F32)
    mix += jnp.dot(b_ref[...].astype(BF16), w_ref[1], preferred_element_type=F32)
    mix += jnp.dot(c_ref[...].astype(BF16), w_ref[2], preferred_element_type=F32)
    mix += jnp.dot(d_ref[...].astype(BF16), w_ref[3], preferred_element_type=F32)
    o_ref[...] = x_ref[...] + _rms(mix, g_ref[...])


def _out_proj(x, o_a, o_b, o_c, o_d, w, g, *, tm=272):
    m, d = x.shape
    tok = pl.BlockSpec((tm, GROUP_W), lambda i: (i, 0))
    return pl.pallas_call(
        _out_proj_body,
        grid=(m // tm,),
        in_specs=[
            pl.BlockSpec((tm, d), lambda i: (i, 0)),
            tok, tok, tok, tok,
            pl.BlockSpec((4, GROUP_W, d), lambda i: (0, 0, 0)),
            pl.BlockSpec((1, d), lambda i: (0, 0)),
        ],
        out_specs=pl.BlockSpec((tm, d), lambda i: (i, 0)),
        out_shape=jax.ShapeDtypeStruct((m, d), F32),
        compiler_params=pltpu.CompilerParams(
            dimension_semantics=("parallel",), vmem_limit_bytes=VMEM_LIMIT),
        name="out_proj",
    )(x, o_a, o_b, o_c, o_d, w, g.reshape(1, d))


def _rms_j(x, g):
    x32 = x.astype(F32)
    return x32 * lax.rsqrt(jnp.mean(x32 * x32, axis=-1, keepdims=True) + EPS) * g.astype(F32)


def _masked_softmax(s, mask):
    s = jnp.where(mask, s.astype(F32), NEG_INF)
    e = jnp.exp(s - jnp.max(s, axis=-1, keepdims=True)) * mask
    return e / jnp.maximum(jnp.sum(e, axis=-1, keepdims=True), TINY)


def _alibi(n):
    return 2.0 ** (-8.0 * jnp.arange(1, n + 1, dtype=F32) / n)


def _rope(x, pos):
    half = x.shape[-1] // 2
    inv = ROPE_THETA ** (-jnp.arange(half, dtype=F32) / half)
    ang = pos.astype(F32)[:, None] * inv
    ang = ang.reshape(ang.shape[0], *([1] * (x.ndim - 3)), half)
    cos, sin = jnp.cos(ang), jnp.sin(ang)
    x1, x2 = x[..., :half], x[..., half:]
    return jnp.concatenate([x1 * cos - x2 * sin, x1 * sin + x2 * cos], axis=-1)


def _over_q(fn, q_pos, *q_args):
    T = q_pos.shape[0]
    blk = Q_BLOCK if T % Q_BLOCK == 0 else T
    nb = T // blk

    def split(a):
        return jnp.moveaxis(a.reshape(a.shape[0], nb, blk, *a.shape[2:]), 1, 0)

    out = lax.map(lambda args: fn(*args), (q_pos.reshape(nb, blk),) + tuple(split(a) for a in q_args))
    out = jnp.moveaxis(out, 0, 1)
    return out.reshape(out.shape[0], T, *out.shape[3:])


def _diff_attention(q, k, v, q_pos, k_pos, lam, lam_init, subln_g):
    slopes = _alibi(DIFF_HEADS)[:, None, None]

    def block(pb, qb):
        s = jnp.einsum('bqhcd,bkcd->bchqk', qb, k) * DIFF_DH ** -0.5
        dist = pb[:, None] - k_pos[None, :]
        s = s - slopes * dist.astype(F32)
        p = _masked_softmax(s, dist >= 0)
        w = p[:, 0] - lam * p[:, 1]
        return jnp.einsum('bhqk,bkd->bqhd', w, v)

    o = _over_q(block, q_pos, q)
    o = _rms_j(o, subln_g) * (1.0 - lam_init)
    return o.reshape(o.shape[0], o.shape[1], -1)


def _mla_attention(q_lat, q_rope, ckv, krope, q_pos, k_pos):
    scale = (MLA_D_NOPE + MLA_D_ROPE) ** -0.5

    def block(pb, qlb, qrb):
        s = (jnp.einsum('bqhr,bkr->bhqk', qlb, ckv) + jnp.einsum('bqhd,bkd->bhqk', qrb, krope)) * scale
        p = _masked_softmax(s, k_pos[None, :] <= pb[:, None])
        return jnp.einsum('bhqk,bkr->bqhr', p, ckv)

    return _over_q(block, q_pos, q_lat, q_rope)


def _nsa_attention(q, gates, kc_seq, vc_seq, ks_seq, vs_seq, kw_seq, vw_seq, q_pos, w_start):
    B = q.shape[0]
    Tk = kc_seq.shape[1]
    nblk = -(-Tk // NSA_BLOCK)
    pad = nblk * NSA_BLOCK - Tk

    def blocks(a):
        return jnp.pad(a, ((0, 0), (0, pad), (0, 0))).reshape(B, nblk, NSA_BLOCK, NSA_DH)

    kc = blocks(kc_seq).mean(axis=2)
    vc = blocks(vc_seq).mean(axis=2)
    ks, vs = blocks(ks_seq), blocks(vs_seq)
    kw = jnp.pad(kw_seq, ((0, 0), (NSA_WINDOW, 0), (0, 0)))
    vw = jnp.pad(vw_seq, ((0, 0), (NSA_WINDOW, 0), (0, 0)))
    blk_ids = jnp.arange(nblk)
    blk_end = (blk_ids + 1) * NSA_BLOCK - 1
    offs = jnp.arange(NSA_BLOCK)
    n_sel = min(NSA_TOPK, nblk)
    slopes = _alibi(NSA_HEADS)[:, None, None]
    scale = NSA_DH ** -0.5

    def block(pb, qb, gb):
        nq = pb.shape[0]
        dist_c = pb[:, None] - blk_end[None, :]
        s_c = jnp.einsum('bqhd,bnd->bhqn', qb, kc) * scale - slopes * dist_c
        p_c = _masked_softmax(s_c, dist_c >= 0)
        o_c = jnp.einsum('bhqn,bnd->bqhd', p_c, vc)
        cur = pb // NSA_BLOCK
        imp = p_c.sum(axis=1)
        forced = (blk_ids[None] == 0) | (blk_ids[None] == cur[:, None]) | (blk_ids[None] == cur[:, None] - 1)
        valid = blk_ids[None] <= cur[:, None]
        score = jnp.where(valid, jnp.where(forced, FORCE_SCORE, imp), -1.0)
        top_val, sel = lax.top_k(score, n_sel)
        ks_sel = jax.vmap(lambda a, i: a[i])(ks, sel).reshape(B, nq, n_sel * NSA_BLOCK, NSA_DH)
        vs_sel = jax.vmap(lambda a, i: a[i])(vs, sel).reshape(B, nq, n_sel * NSA_BLOCK, NSA_DH)
        sel_pos = sel[..., None] * NSA_BLOCK + offs
        dist_s = (pb[None, :, None, None] - sel_pos).reshape(B, nq, n_sel * NSA_BLOCK)
        mask_s = ((dist_s >= 0) & jnp.repeat(top_val >= 0, NSA_BLOCK, axis=-1))[:, None]
        s_s = jnp.einsum('bqhd,bqkd->bhqk', qb, ks_sel) * scale - slopes * dist_s[:, None]
        p_s = _masked_softmax(s_s, mask_s)
        o_s = jnp.einsum('bhqk,bqkd->bqhd', p_s, vs_sel)
        r0 = pb[0] - w_start
        kwb = lax.dynamic_slice_in_dim(kw, r0, NSA_WINDOW + nq, axis=1)
        vwb = lax.dynamic_slice_in_dim(vw, r0, NSA_WINDOW + nq, axis=1)
        w_pos = pb[0] - NSA_WINDOW + jnp.arange(NSA_WINDOW + nq)
        dist_w = pb[:, None] - w_pos[None, :]
        mask_w = (dist_w >= 0) & (dist_w < NSA_WINDOW) & (w_pos[None, :] >= w_start)
        s_w = jnp.einsum('bqhd,bkd->bhqk', qb, kwb) * scale - slopes * dist_w
        p_w = _masked_softmax(s_w, mask_w)
        o_w = jnp.einsum('bhqk,bkd->bqhd', p_w, vwb)
        g = jax.nn.sigmoid(gb.astype(F32))
        return g[..., 0:1] * o_c + g[..., 1:2] * o_s + g[..., 2:3] * o_w

    return _over_q(block, q_pos, q, gates)


def _ssd_scan(x, dt, A, Bm, Cm, h0):
    B, T, H, P = x.shape
    Q = SSD_CHUNK if T % SSD_CHUNK == 0 else T
    nc = T // Q

    def chunks(a):
        return jnp.moveaxis(a.reshape(B, nc, Q, *a.shape[2:]), 1, 0)

    causal = jnp.tril(jnp.ones((Q, Q), bool))[None, :, :, None]

    def step(h, inp):
        xc, dtc, Bc, Cc = inp
        acum = jnp.cumsum(dtc * A, axis=1)
        seg = acum[:, :, None, :] - acum[:, None, :, :]
        decay = jnp.where(causal, jnp.exp(jnp.where(causal, seg, 0.0)), 0.0)
        w = jnp.einsum('bihn,bjhn->bijh', Cc, Bc) * decay * dtc[:, None, :, :]
        y = (jnp.einsum('bijh,bjhp->bihp', w, xc)
             + jnp.einsum('bihn,bhpn->bihp', Cc, h) * jnp.exp(acum)[..., None])
        to_end = jnp.exp(acum[:, -1:, :] - acum) * dtc
        h = h * jnp.exp(acum[:, -1])[:, :, None, None] + jnp.einsum('bjh,bjhn,bjhp->bhpn', to_end, Bc, xc)
        return h, y

    h, ys = lax.scan(step, h0, (chunks(x), chunks(dt), chunks(Bm), chunks(Cm)))
    return jnp.moveaxis(ys, 0, 1).reshape(B, T, H, P), h


def _ssd_mixer(z, xbc, dt_raw, conv_state, h0, conv_w, conv_b, dt_bias, a_log, d_skip, norm_g):
    B, T, _ = xbc.shape
    xpad = jnp.concatenate([conv_state, xbc], axis=1)
    new_conv = xpad[:, -(SSD_CONV - 1):]
    conv = lax.conv_general_dilated(xpad, conv_w[:, None, :], window_strides=(1,),
                                    padding='VALID', dimension_numbers=('NWC', 'WIO', 'NWC'),
                                    feature_group_count=SSD_CONV_CH) + conv_b
    xbc = jax.nn.silu(conv)
    xs, Bm, Cm = jnp.split(xbc, [GROUP_W, GROUP_W + SSD_GROUPS * SSD_STATE], axis=-1)
    xs = xs.reshape(B, T, SSD_HEADS, SSD_HEADDIM)
    rep = SSD_HEADS // SSD_GROUPS
    Bm = jnp.repeat(Bm.reshape(B, T, SSD_GROUPS, SSD_STATE), rep, axis=2)
    Cm = jnp.repeat(Cm.reshape(B, T, SSD_GROUPS, SSD_STATE), rep, axis=2)
    dt = jax.nn.softplus(dt_raw + dt_bias)
    A = -jnp.exp(a_log)
    y, h = _ssd_scan(xs, dt, A, Bm, Cm, h0)
    y = y + d_skip[:, None] * xs
    y = y.reshape(B, T, GROUP_W) * jax.nn.silu(z)
    return _rms_j(y, norm_g), new_conv, h


def _mixers(u, l, p, past, past_len, w_buf):
    B, T, _ = u.shape
    pos = past_len + jnp.arange(T)
    dq = u[..., COL_DQ:COL_DQ + 512]
    nq = u[..., COL_NQ:COL_NQ + 512]
    sz = u[..., COL_SZ:COL_SZ + 512]
    new_diff = u[..., COL_DK:COL_DK + 256]
    mckv = u[..., COL_MCKV:COL_MCKV + 128]
    mkr = u[..., COL_SMALL:COL_SMALL + 32]
    ngate = u[..., COL_SMALL + 32:COL_SMALL + 56]
    sdt = u[..., COL_SMALL + 56:COL_SMALL + 64]
    sxbc = u[..., COL_SXBC:COL_SXBC + 1024]
    mcq = u[..., COL_MCQ:COL_MCQ + 384]
    nkv = u[..., COL_NKV:COL_NKV + 384]

    new_mla = jnp.concatenate([_rms_j(mckv, p['mla_kv_norm_g']), _rope(mkr, pos)], axis=-1)
    new_nsa = nkv[..., :4 * NSA_DH]
    new_win_rows = nkv[..., 4 * NSA_DH:]

    if past is None:
        full_diff, full_mla, full_nsa, win_seq = new_diff, new_mla, new_nsa, new_win_rows
        w_start = 0
        conv_state = jnp.zeros((B, SSD_CONV - 1, SSD_CONV_CH), F32)
        h0 = jnp.zeros((B, SSD_HEADS, SSD_HEADDIM, SSD_STATE), F32)
    else:
        full_diff = jnp.concatenate([past['diff'], new_diff], axis=1)
        full_mla = jnp.concatenate([past['mla'], new_mla], axis=1)
        full_nsa = jnp.concatenate([past['nsa'], new_nsa], axis=1)
        win_seq = jnp.concatenate([past['win'], new_win_rows], axis=1)
        w_start = past_len - past['win'].shape[1]
        conv_state, h0 = past['conv'], past['h']
    Tk = full_diff.shape[1]
    k_pos = jnp.arange(Tk)
    new_win = jnp.pad(win_seq, ((0, 0), (max(0, w_buf - win_seq.shape[1]), 0), (0, 0)))[:, -w_buf:]

    lv = p['diff_lambda']
    lam_init = 0.8 - 0.6 * math.exp(-0.3 * l)
    lam = jnp.exp(jnp.sum(lv[0] * lv[1])) - jnp.exp(jnp.sum(lv[2] * lv[3])) + lam_init
    o_a = _diff_attention(dq.reshape(B, T, DIFF_HEADS, 2, DIFF_DH),
                          full_diff[..., :2 * DIFF_DH].reshape(B, Tk, 2, DIFF_DH),
                          full_diff[..., 2 * DIFF_DH:], pos, k_pos, lam, lam_init, p['diff_subln_g'])

    qm = (_rms_j(mcq, p['mla_q_norm_g']) @ p['mla_w_uq']).reshape(B, T, MLA_HEADS, MLA_D_NOPE + MLA_D_ROPE)
    q_nope, q_rope = qm[..., :MLA_D_NOPE], _rope(qm[..., MLA_D_NOPE:], pos)
    q_lat = jnp.einsum('bqhn,rhn->bqhr', q_nope, p['mla_w_uk'])
    o_lat = _mla_attention(q_lat, q_rope, full_mla[..., :MLA_KV_RANK], full_mla[..., MLA_KV_RANK:], pos, k_pos)
    o_b = jnp.einsum('bqhr,rhd->bqhd', o_lat, p['mla_w_uv']).reshape(B, T, GROUP_W)
    o_b = _rms_j(o_b, p['mla_out_g'])

    o_c = _nsa_attention(nq.reshape(B, T, NSA_HEADS, NSA_DH), ngate.reshape(B, T, NSA_HEADS, 3),
                         full_nsa[..., :NSA_DH], full_nsa[..., NSA_DH:2 * NSA_DH],
                         full_nsa[..., 2 * NSA_DH:3 * NSA_DH], full_nsa[..., 3 * NSA_DH:],
                         win_seq[..., :NSA_DH], win_seq[..., NSA_DH:], pos, w_start)
    o_c = _rms_j(o_c.reshape(B, T, GROUP_W), p['nsa_out_g'])

    o_d, new_conv, new_h = _ssd_mixer(sz, sxbc, sdt, conv_state, h0, p['ssd_conv_w'], p['ssd_conv_b'],
                                      p['ssd_dt_bias'], p['ssd_a_log'], p['ssd_d'], p['ssd_norm_g'])
    return (o_a, o_b, o_c, o_d), (new_diff, new_mla, new_nsa, new_win, new_conv, new_h)


def _permute_w_in(w_in):
    def sl(a, b):
        return w_in[..., a:b]
    z64 = jnp.zeros(w_in.shape[:-1] + (64,), w_in.dtype)
    return jnp.concatenate([
        sl(0, 512), sl(1312, 1824), sl(2232, 2744), sl(512, 640), sl(640, 768), sl(1152, 1280),
        sl(1280, 1312), sl(2208, 2232), sl(3768, 3776), z64,
        sl(2744, 3768), sl(768, 1152), sl(1824, 2208)], axis=-1)


def kernel(x_prompt, x_sample, cache_diff_kv, cache_mla, cache_nsa_kv, cache_nsa_win, state_ssd_conv, state_ssd_h, page_table, ffn1_pre_g, ffn1_post_g, ffn1_w1, ffn1_w2, mix_pre_g, mix_post_g, w_in, w_out, diff_lambda, diff_subln_g, mla_q_norm_g, mla_w_uq, mla_kv_norm_g, mla_w_uk, mla_w_uv, mla_out_g, nsa_out_g, ssd_conv_w, ssd_conv_b, ssd_dt_bias, ssd_a_log, ssd_d, ssd_norm_g, ffn2_pre_g, ffn2_post_g, ffn2_w1, ffn2_w2):
    depth = w_in.shape[0]
    bp, tp, d = x_prompt.shape
    bs, ts, _ = x_sample.shape
    mp, ms = bp * tp, bs * ts
    past_len = page_table.shape[1] * cache_diff_kv.shape[2]
    w_buf = cache_nsa_win.shape[2]

    w_in_p = _permute_w_in(w_in).astype(BF16)
    w_out_b = w_out.astype(BF16).reshape(depth, 4, GROUP_W, d)
    f1w1, f1w2 = ffn1_w1.astype(BF16), ffn1_w2.astype(BF16)
    f2w1, f2w2 = ffn2_w1.astype(BF16), ffn2_w2.astype(BF16)

    mix_names = dict(diff_lambda=diff_lambda, diff_subln_g=diff_subln_g, mla_q_norm_g=mla_q_norm_g,
                     mla_w_uq=mla_w_uq, mla_kv_norm_g=mla_kv_norm_g, mla_w_uk=mla_w_uk, mla_w_uv=mla_w_uv,
                     mla_out_g=mla_out_g, nsa_out_g=nsa_out_g, ssd_conv_w=ssd_conv_w, ssd_conv_b=ssd_conv_b,
                     ssd_dt_bias=ssd_dt_bias, ssd_a_log=ssd_a_log, ssd_d=ssd_d, ssd_norm_g=ssd_norm_g)

    x = jnp.concatenate([x_prompt.reshape(mp, d), x_sample.reshape(ms, d)], axis=0)
    st_p, st_s = [], []
    for l in range(depth):
        p = {k: v[l] for k, v in mix_names.items()}
        x = _ffn(x, ffn1_pre_g[l], ffn1_post_g[l], f1w1[l], f1w2[l])
        u = _in_proj(x, mix_pre_g[l], w_in_p[l])
        o_p, new_p = _mixers(u[:mp].reshape(bp, tp, -1), l, p, None, 0, w_buf)
        past = dict(diff=cache_diff_kv[l, page_table].reshape(bs, past_len, -1),
                    mla=cache_mla[l, page_table].reshape(bs, past_len, -1),
                    nsa=cache_nsa_kv[l, page_table].reshape(bs, past_len, -1),
                    win=cache_nsa_win[l], conv=state_ssd_conv[l], h=state_ssd_h[l])
        o_s, new_s = _mixers(u[mp:].reshape(bs, ts, -1), l, p, past, past_len, w_buf)
        o = [jnp.concatenate([a.reshape(mp, GROUP_W), b.reshape(ms, GROUP_W)], axis=0)
             for a, b in zip(o_p, o_s)]
        x = _out_proj(x, o[0], o[1], o[2], o[3], w_out_b[l], mix_post_g[l])
        x = _ffn(x, ffn2_pre_g[l], ffn2_post_g[l], f2w1[l], f2w2[l])
        st_p.append(new_p)
        st_s.append(new_s)

    def stacked(states, i):
        return jnp.stack([s[i] for s in states])

    return (x[:mp].reshape(bp, tp, d), x[mp:].reshape(bs, ts, d),
            stacked(st_p, 0), stacked(st_s, 0),
            stacked(st_p, 1), stacked(st_s, 1),
            stacked(st_p, 2), stacked(st_s, 2),
            stacked(st_p, 3), stacked(st_s, 3),
            stacked(st_p, 4), stacked(st_s, 4),
            stacked(st_p, 5), stacked(st_s, 5))
```

```python
import functools
import math

import numpy as np
import jax
import jax.numpy as jnp
from jax import lax
from jax.experimental import pallas as pl
from jax.experimental.pallas import tpu as pltpu

D_MODEL = 2048
GROUP_W = 512
DIFF_HEADS = 4
DIFF_DH = 64
DIFF_DV = 128
MLA_HEADS = 4
MLA_DV = 128
MLA_D_NOPE = 128
MLA_D_ROPE = 32
MLA_Q_RANK = 384
MLA_KV_RANK = 128
ROPE_THETA = 10000.0
NSA_HEADS = 8
NSA_DH = 64
NSA_BLOCK = 64
NSA_TOPK = 16
NSA_WINDOW = 512
FORCE_SCORE = 1.0e4
SSD_HEADDIM = 64
SSD_HEADS = 8
SSD_STATE = 128
SSD_GROUPS = 2
SSD_CONV = 4
SSD_CHUNK = 128
SSD_CONV_CH = 1024
D_FF = 5632
Q_BLOCK = 128
EPS = 1e-6
NEG_INF = -1e30
TINY = 1e-30
F32 = jnp.float32
BF16 = jnp.bfloat16

COL_DQ = 0
COL_NQ = 512
COL_SZ = 1024
COL_DK = 1536
COL_DV = 1664
COL_MCKV = 1792
COL_SMALL = 1920
COL_SXBC = 2048
COL_MCQ = 3072
COL_NKV = 3456
N_IN_PAD = 3840

VMEM_LIMIT = 56 * 1024 * 1024


def _rms(x, g):
    return x * lax.rsqrt(jnp.mean(x * x, axis=-1, keepdims=True) + EPS) * g


def _ffn_body(x_ref, pre_ref, post_ref, wg_ref, wu_ref, w2_ref, o_ref, xn_ref, acc_ref):
    j = pl.program_id(1)

    @pl.when(j == 0)
    def _():
        xn_ref[...] = _rms(x_ref[...], pre_ref[...]).astype(BF16)
        acc_ref[...] = jnp.zeros_like(acc_ref)

    xn = xn_ref[...]
    g = jnp.dot(xn, wg_ref[...], preferred_element_type=F32)
    u = jnp.dot(xn, wu_ref[...], preferred_element_type=F32)
    h = (g * jax.nn.sigmoid(g) * u).astype(BF16)
    acc_ref[...] += jnp.dot(h, w2_ref[...], preferred_element_type=F32)

    @pl.when(j == pl.num_programs(1) - 1)
    def _():
        o_ref[...] = x_ref[...] + 0.5 * _rms(acc_ref[...], post_ref[...])


def _ffn(x, pre_g, post_g, w1, w2, *, tm=544, tf=512):
    m, d = x.shape
    nf = D_FF // tf
    return pl.pallas_call(
        _ffn_body,
        grid=(m // tm, nf),
        in_specs=[
            pl.BlockSpec((tm, d), lambda i, j: (i, 0)),
            pl.BlockSpec((1, d), lambda i, j: (0, 0)),
            pl.BlockSpec((1, d), lambda i, j: (0, 0)),
            pl.BlockSpec((d, tf), lambda i, j: (0, j)),
            pl.BlockSpec((d, tf), lambda i, j: (0, j + nf)),
            pl.BlockSpec((tf, d), lambda i, j: (j, 0)),
        ],
        out_specs=pl.BlockSpec((tm, d), lambda i, j: (i, 0)),
        out_shape=jax.ShapeDtypeStruct((m, d), F32),
        scratch_shapes=[pltpu.VMEM((tm, d), BF16), pltpu.VMEM((tm, d), F32)],
        compiler_params=pltpu.CompilerParams(
            dimension_semantics=("parallel", "arbitrary"), vmem_limit_bytes=VMEM_LIMIT),
        name="ffn",
    )(x, pre_g.reshape(1, d), post_g.reshape(1, d), w1, w1, w2)


def _in_proj_body(x_ref, g_ref, w_ref, o_ref, xn_ref):
    @pl.when(pl.program_id(1) == 0)
    def _():
        xn_ref[...] = _rms(x_ref[...], g_ref[...]).astype(BF16)

    o_ref[...] = jnp.dot(xn_ref[...], w_ref[...], preferred_element_type=F32)


def _in_proj(x, g, w, *, tm=544, tn=1280):
    m, d = x.shape
    n = w.shape[1]
    return pl.pallas_call(
        _in_proj_body,
        grid=(m // tm, n // tn),
        in_specs=[
            pl.BlockSpec((tm, d), lambda i, j: (i, 0)),
            pl.BlockSpec((1, d), lambda i, j: (0, 0)),
            pl.BlockSpec((d, tn), lambda i, j: (0, j)),
        ],
        out_specs=pl.BlockSpec((tm, tn), lambda i, j: (i, j)),
        out_shape=jax.ShapeDtypeStruct((m, n), F32),
        scratch_shapes=[pltpu.VMEM((tm, d), BF16)],
        compiler_params=pltpu.CompilerParams(
            dimension_semantics=("parallel", "arbitrary"), vmem_limit_bytes=VMEM_LIMIT),
        name="in_proj",
    )(x, g.reshape(1, d), w)


def _out_proj_body(x_ref, a_ref, b_ref, c_ref, d_ref, w_ref, g_ref, o_ref):
    mix = jnp.dot(a_ref[...].astype(BF16), w_ref[0], preferred_element_type=F32)
    mix += jnp.dot(b_ref[...].astype(BF16), w_ref[1], preferred_element_type=F32)
    mix += jnp.dot(c_ref[...].astype(BF16), w_ref[2], preferred_element_type=F32)
    mix += jnp.dot(d_ref[...].astype(BF16), w_ref[3], preferred_element_type=F32)
    o_ref[...] = x_ref[...] + _rms(mix, g_ref[...])


def _out_proj(x, o_a, o_b, o_c, o_d, w, g, *, tm=272):
    m, d = x.shape
    tok = pl.BlockSpec((tm, GROUP_W), lambda i: (i, 0))
    return pl.pallas_call(
        _out_proj_body,
        grid=(m // tm,),
        in_specs=[
            pl.BlockSpec((tm, d), lambda i: (i, 0)),
            tok, tok, tok, tok,
            pl.BlockSpec((4, GROUP_W, d), lambda i: (0, 0, 0)),
            pl.BlockSpec((1, d), lambda i: (0, 0)),
        ],
        out_specs=pl.BlockSpec((tm, d), lambda i: (i, 0)),
        out_shape=jax.ShapeDtypeStruct((m, d), F32),
        compiler_params=pltpu.CompilerParams(
            dimension_semantics=("parallel",), vmem_limit_bytes=VMEM_LIMIT),
        name="out_proj",
    )(x, o_a, o_b, o_c, o_d, w, g.reshape(1, d))


def _softmax_step(sc, mask, m_ref, l_ref, idx):
    sc = jnp.where(mask, sc, NEG_INF)
    m_old = m_ref[idx]
    m_new = jnp.maximum(m_old, jnp.max(sc, axis=-1, keepdims=True))
    p = jnp.where(mask, jnp.exp(sc - m_new), 0.0)
    alpha = jnp.exp(m_old - m_new)
    l_ref[idx] = alpha * l_ref[idx] + jnp.sum(p, axis=-1, keepdims=True)
    m_ref[idx] = m_new
    return p, alpha


def _dot_nt(a, b):
    return lax.dot_general(a, b, (((1,), (1,)), ((), ())), preferred_element_type=F32)


def _diff_lambda(lv_ref, lam_init):
    lv = lv_ref[...]
    a = jnp.sum(lv[0:1] * lv[1:2], axis=-1, keepdims=True)
    b = jnp.sum(lv[2:3] * lv[3:4], axis=-1, keepdims=True)
    return jnp.exp(a) - jnp.exp(b) + lam_init


def _diff_streams(q, rows):
    lane = lax.broadcasted_iota(jnp.int32, (rows, 128), 1)
    parts = []
    for c in range(2):
        keep = (lane < DIFF_DH) if c == 0 else (lane >= DIFF_DH)
        for h in range(DIFF_HEADS):
            parts.append(jnp.where(keep, q[:, h * 128:(h + 1) * 128], 0.0))
    return jnp.concatenate(parts, axis=0).astype(BF16)


def _diff_prompt_body(lv_ref, q_ref, k_ref, v_ref, g_ref, o_ref, m_ref, l_ref, acc_ref, *, tq, tk, lam_init):
    qi = pl.program_id(1)
    q0 = qi * tq
    qst = _diff_streams(q_ref[...], tq)
    m_ref[...] = jnp.full_like(m_ref, NEG_INF)
    l_ref[...] = jnp.zeros_like(l_ref)
    acc_ref[...] = jnp.zeros_like(acc_ref)
    slopes = [2.0 ** (-8.0 * (h + 1) / DIFF_HEADS) for h in range(DIFF_HEADS)]
    ri = lax.broadcasted_iota(jnp.int32, (tq, tk), 0)
    ci = lax.broadcasted_iota(jnp.int32, (tq, tk), 1)

    def chunk(kc, carry):
        k0 = pl.multiple_of(kc * tk, tk)
        kb = k_ref[pl.ds(k0, tk), :].astype(BF16)
        vb = v_ref[pl.ds(k0, tk), :].astype(BF16)
        s_all = _dot_nt(qst, kb)
        dist = (q0 - k0) + ri - ci
        mask = dist >= 0
        distf = dist.astype(F32)
        ps, alphas = [], []
        for s in range(2 * DIFF_HEADS):
            sc = s_all[s * tq:(s + 1) * tq] * (DIFF_DH ** -0.5) - slopes[s % DIFF_HEADS] * distf
            p, alpha = _softmax_step(sc, mask, m_ref, l_ref, s)
            ps.append(p.astype(BF16))
            alphas.append(alpha)
        pv = jnp.dot(jnp.concatenate(ps, axis=0), vb, preferred_element_type=F32)
        for s in range(2 * DIFF_HEADS):
            acc_ref[s] = alphas[s] * acc_ref[s] + pv[s * tq:(s + 1) * tq]
        return carry

    lax.fori_loop(0, (q0 + tq + tk - 1) // tk, chunk, 0)
    lam = _diff_lambda(lv_ref, lam_init)
    for h in range(DIFF_HEADS):
        o0 = acc_ref[h] / jnp.maximum(l_ref[h], TINY)
        o1 = acc_ref[DIFF_HEADS + h] / jnp.maximum(l_ref[DIFF_HEADS + h], TINY)
        o_ref[:, h * 128:(h + 1) * 128] = _rms(o0 - lam * o1, g_ref[...]) * (1.0 - lam_init)


def _diff_prompt(u, lv, subln_g, *, b, t, lam_init, tq=128, tk=256):
    nq = t // tq
    return pl.pallas_call(
        functools.partial(_diff_prompt_body, tq=tq, tk=tk, lam_init=lam_init),
        grid=(b, nq),
        in_specs=[
            pl.BlockSpec((4, DIFF_DH), lambda bi, qi: (0, 0)),
            pl.BlockSpec((tq, 512), lambda bi, qi: (bi * nq + qi, COL_DQ // 512)),
            pl.BlockSpec((t, 128), lambda bi, qi: (bi, COL_DK // 128)),
            pl.BlockSpec((t, 128), lambda bi, qi: (bi, COL_DV // 128)),
            pl.BlockSpec((1, DIFF_DV), lambda bi, qi: (0, 0)),
        ],
        out_specs=pl.BlockSpec((tq, GROUP_W), lambda bi, qi: (bi * nq + qi, 0)),
        out_shape=jax.ShapeDtypeStruct((b * t, GROUP_W), F32),
        scratch_shapes=[pltpu.VMEM((8, tq, 1), F32), pltpu.VMEM((8, tq, 1), F32),
                        pltpu.VMEM((8, tq, 128), F32)],
        compiler_params=pltpu.CompilerParams(
            dimension_semantics=("parallel", "parallel"), vmem_limit_bytes=VMEM_LIMIT),
        name="diff_prompt",
    )(lv, u, u, u, subln_g.reshape(1, DIFF_DV))


def _rope_tables(pos):
    half = MLA_D_ROPE // 2
    inv = ROPE_THETA ** (-jnp.arange(half, dtype=F32) / half)
    ang = pos.astype(F32)[:, None] * inv
    cos, sin = jnp.cos(ang), jnp.sin(ang)
    cos_t = jnp.tile(cos, (1, 128 // half))
    sin_t = jnp.tile(jnp.concatenate([-sin, sin], axis=-1), (1, 128 // MLA_D_ROPE))
    return cos_t, sin_t


def _rope_lanes(x, cos_t, sin_t):
    lane = lax.broadcasted_iota(jnp.int32, x.shape, 1)
    first = (lane % MLA_D_ROPE) < (MLA_D_ROPE // 2)
    partner = jnp.where(first, pltpu.roll(x, 128 - MLA_D_ROPE // 2, 1), pltpu.roll(x, MLA_D_ROPE // 2, 1))
    return x * cos_t + partner * sin_t


def _mla_weight_layout(w_uq, w_uk, w_uv):
    w = w_uq.reshape(MLA_Q_RANK, MLA_HEADS, MLA_D_NOPE + MLA_D_ROPE)
    wuq = jnp.concatenate([w[:, :, :MLA_D_NOPE].reshape(MLA_Q_RANK, -1),
                           w[:, :, MLA_D_NOPE:].reshape(MLA_Q_RANK, -1)], axis=-1)
    return wuq.astype(BF16), w_uk.transpose(1, 2, 0).astype(BF16), w_uv.transpose(1, 0, 2).astype(BF16)


def _mla_prep_body(cq_ref, ckv_ref, sm_ref, cos_ref, sin_ref, gq_ref, gkv_ref, wuq_ref, wuk_ref, q_ref, k_ref):
    tm = cq_ref.shape[0]
    cos_t, sin_t = cos_ref[...], sin_ref[...]
    lane = lax.broadcasted_iota(jnp.int32, (tm, 128), 1)
    low = lane < MLA_D_ROPE
    k_ref[:, 0:128] = _rms(ckv_ref[...], gkv_ref[...])
    k_ref[:, 128:256] = jnp.where(low, _rope_lanes(sm_ref[...], cos_t, sin_t), 0.0)
    qm = jnp.dot(_rms(cq_ref[...], gq_ref[...]).astype(BF16), wuq_ref[...], preferred_element_type=F32)
    qr = _rope_lanes(qm[:, MLA_HEADS * MLA_D_NOPE:], cos_t, sin_t)
    for h in range(MLA_HEADS):
        qn = qm[:, h * MLA_D_NOPE:(h + 1) * MLA_D_NOPE].astype(BF16)
        q_ref[:, h * 256:h * 256 + 128] = jnp.dot(qn, wuk_ref[h], preferred_element_type=F32).astype(BF16)
        qr_h = qr if h == 0 else pltpu.roll(qr, 128 - h * MLA_D_ROPE, 1)
        q_ref[:, h * 256 + 128:(h + 1) * 256] = jnp.where(low, qr_h, 0.0).astype(BF16)


def _mla_prep(u, cos_t, sin_t, gq, gkv, wuq, wuk, *, tm=544):
    m = u.shape[0]
    row = lambda i: (i, 0)
    fixed2 = lambda i: (0, 0)
    return pl.pallas_call(
        _mla_prep_body,
        grid=(m // tm,),
        in_specs=[
            pl.BlockSpec((tm, MLA_Q_RANK), lambda i: (i, COL_MCQ // MLA_Q_RANK)),
            pl.BlockSpec((tm, 128), lambda i: (i, COL_MCKV // 128)),
            pl.BlockSpec((tm, 128), lambda i: (i, COL_SMALL // 128)),
            pl.BlockSpec((tm, 128), row),
            pl.BlockSpec((tm, 128), row),
            pl.BlockSpec((1, MLA_Q_RANK), fixed2),
            pl.BlockSpec((1, MLA_KV_RANK), fixed2),
            pl.BlockSpec((MLA_Q_RANK, 640), fixed2),
            pl.BlockSpec((MLA_HEADS, 128, 128), lambda i: (0, 0, 0)),
        ],
        out_specs=[pl.BlockSpec((tm, MLA_HEADS * 256), row), pl.BlockSpec((tm, 256), row)],
        out_shape=[jax.ShapeDtypeStruct((m, MLA_HEADS * 256), BF16), jax.ShapeDtypeStruct((m, 256), F32)],
        compiler_params=pltpu.CompilerParams(dimension_semantics=("parallel",), vmem_limit_bytes=VMEM_LIMIT),
        name="mla_prep",
    )(u, u, u, cos_t, sin_t, gq.reshape(1, -1), gkv.reshape(1, -1), wuq, wuk)


def _mla_prompt_body(q_ref, k_ref, wuv_ref, g_ref, o_ref, m_ref, l_ref, acc_ref, *, tq, tk):
    qi = pl.program_id(1)
    q0 = qi * tq
    qst = jnp.concatenate([q_ref[:, h * 256:(h + 1) * 256] for h in range(MLA_HEADS)], axis=0)
    m_ref[...] = jnp.full_like(m_ref, NEG_INF)
    l_ref[...] = jnp.zeros_like(l_ref)
    acc_ref[...] = jnp.zeros_like(acc_ref)
    ri = lax.broadcasted_iota(jnp.int32, (tq, tk), 0)
    ci = lax.broadcasted_iota(jnp.int32, (tq, tk), 1)
    scale = (MLA_D_NOPE + MLA_D_ROPE) ** -0.5

    def chunk(kc, carry):
        k0 = pl.multiple_of(kc * tk, tk)
        kb = k_ref[pl.ds(k0, tk), :].astype(BF16)
        s_all = _dot_nt(qst, kb)
        mask = ((q0 - k0) + ri - ci) >= 0
        ps, alphas = [], []
        for h in range(MLA_HEADS):
            p, alpha = _softmax_step(s_all[h * tq:(h + 1) * tq] * scale, mask, m_ref, l_ref, h)
            ps.append(p.astype(BF16))
            alphas.append(alpha)
        pv = jnp.dot(jnp.concatenate(ps, axis=0), kb[:, 0:MLA_KV_RANK], preferred_element_type=F32)
        for h in range(MLA_HEADS):
            acc_ref[h] = alphas[h] * acc_ref[h] + pv[h * tq:(h + 1) * tq]
        return carry

    lax.fori_loop(0, (q0 + tq + tk - 1) // tk, chunk, 0)
    outs = []
    for h in range(MLA_HEADS):
        o_lat = (acc_ref[h] / jnp.maximum(l_ref[h], TINY)).astype(BF16)
        outs.append(jnp.dot(o_lat, wuv_ref[h], preferred_element_type=F32))
    o_ref[...] = _rms(jnp.concatenate(outs, axis=-1), g_ref[...])


def _mla_prompt(q, k, wuv, g, *, b, t, tq=128, tk=256):
    nq = t // tq
    return pl.pallas_call(
        functools.partial(_mla_prompt_body, tq=tq, tk=tk),
        grid=(b, nq),
        in_specs=[
            pl.BlockSpec((tq, MLA_HEADS * 256), lambda bi, qi: (bi * nq + qi, 0)),
            pl.BlockSpec((t, 256), lambda bi, qi: (bi, 0)),
            pl.BlockSpec((MLA_HEADS, 128, 128), lambda bi, qi: (0, 0, 0)),
            pl.BlockSpec((1, GROUP_W), lambda bi, qi: (0, 0)),
        ],
        out_specs=pl.BlockSpec((tq, GROUP_W), lambda bi, qi: (bi * nq + qi, 0)),
        out_shape=jax.ShapeDtypeStruct((b * t, GROUP_W), F32),
        scratch_shapes=[pltpu.VMEM((MLA_HEADS, tq, 1), F32), pltpu.VMEM((MLA_HEADS, tq, 1), F32),
                        pltpu.VMEM((MLA_HEADS, tq, 128), F32)],
        compiler_params=pltpu.CompilerParams(
            dimension_semantics=("parallel", "parallel"), vmem_limit_bytes=VMEM_LIMIT),
        name="mla_prompt",
    )(q, k, wuv, g.reshape(1, GROUP_W))


def _nsa_slopes():
    return [2.0 ** (-8.0 * (h + 1) / NSA_HEADS) for h in range(NSA_HEADS)]


def _nsa_select(imp, posq, lane):
    cur = lax.shift_right_logical(posq, 6)
    forced = (lane == 0) | (lane == cur) | (lane == cur - 1)
    score = jnp.where(lane <= cur, jnp.where(forced, FORCE_SCORE, imp), -1.0)
    lane_f = lane.astype(F32)
    sel = jnp.zeros(imp.shape, F32)
    for _ in range(NSA_TOPK):
        mx = jnp.max(score, axis=-1, keepdims=True)
        idx = jnp.min(jnp.where(score == mx, lane_f, 1e9), axis=-1, keepdims=True)
        hit = lane_f == idx
        sel = jnp.where(hit & (mx >= 0.0), 1.0, sel)
        score = jnp.where(hit, -2.0, score)
    return sel


def _nsa_prompt_body(q_ref, sm_ref, kv_ref, g_ref, o_ref, kc_ref, vc_ref, ke_ref, ko_ref, vd_ref,
                     m_ref, l_ref, acc_ref, *, t, tq, tk):
    qi = pl.program_id(1)
    q0 = qi * tq
    nb = t // NSA_BLOCK
    slopes = _nsa_slopes()

    @pl.when(qi == 0)
    def _():
        lo_t = lax.broadcasted_iota(jnp.int32, (t, 128), 1) < NSA_DH
        for br in range(2):
            kvb = kv_ref[:, 128 * (br + 1):128 * (br + 2)]
            sw = pltpu.roll(kvb, NSA_DH, 1)
            ke_ref[br] = jnp.where(lo_t, kvb, 0.0).astype(BF16)
            ko_ref[br] = jnp.where(lo_t, 0.0, sw).astype(BF16)
            vd_ref[br] = jnp.where(lo_t, sw, kvb).astype(BF16)
        cm = jnp.sum(kv_ref[:, 0:128].reshape(nb, NSA_BLOCK, 128), axis=1) * (1.0 / NSA_BLOCK)
        cm = jnp.concatenate([cm, jnp.zeros((128 - nb, 128), F32)], axis=0)
        sw = pltpu.roll(cm, NSA_DH, 1)
        lo_c = lax.broadcasted_iota(jnp.int32, (128, 128), 1) < NSA_DH
        kc_ref[0:128, :] = jnp.where(lo_c, cm, 0.0).astype(BF16)
        kc_ref[128:256, :] = jnp.where(lo_c, 0.0, sw).astype(BF16)
        vc_ref[...] = jnp.where(lo_c, sw, cm).astype(BF16)

    q4 = jnp.concatenate([q_ref[:, j * 128:(j + 1) * 128] for j in range(4)], axis=0).astype(BF16)
    lane = lax.broadcasted_iota(jnp.int32, (tq, 128), 1)
    posq = q0 + lax.broadcasted_iota(jnp.int32, (tq, 128), 0)

    s_c = _dot_nt(q4, kc_ref[...])
    dist_c = posq - (NSA_BLOCK * (lane + 1) - 1)
    mask_c = dist_c >= 0
    dist_cf = dist_c.astype(F32)
    imp = jnp.zeros((tq, 128), F32)
    pcs = []
    for h in range(NSA_HEADS):
        j, e = divmod(h, 2)
        sc = s_c[j * tq:(j + 1) * tq, e * 128:(e + 1) * 128] * (NSA_DH ** -0.5) - slopes[h] * dist_cf
        sc = jnp.where(mask_c, sc, NEG_INF)
        ex = jnp.where(mask_c, jnp.exp(sc - jnp.max(sc, axis=-1, keepdims=True)), 0.0)
        pc = ex / jnp.maximum(jnp.sum(ex, axis=-1, keepdims=True), TINY)
        imp = imp + pc
        pcs.append(pc.astype(BF16))
    oc = jnp.dot(jnp.concatenate(pcs, axis=0), vc_ref[...], preferred_element_type=F32)
    sel = _nsa_select(imp, posq, lane).astype(BF16)

    m_ref[...] = jnp.full_like(m_ref, NEG_INF)
    l_ref[...] = jnp.zeros_like(l_ref)
    acc_ref[...] = jnp.zeros_like(acc_ref)
    ri = lax.broadcasted_iota(jnp.int32, (tq, tk), 0)
    ci = lax.broadcasted_iota(jnp.int32, (tq, tk), 1)
    bj = lax.broadcasted_iota(jnp.int32, (128, tk), 0)
    bk = lax.broadcasted_iota(jnp.int32, (128, tk), 1)

    def make_chunk(br):
        def chunk(kc, carry):
            k0 = pl.multiple_of(kc * tk, tk)
            s_e = _dot_nt(q4, ke_ref[br, pl.ds(k0, tk), :])
            s_o = _dot_nt(q4, ko_ref[br, pl.ds(k0, tk), :])
            dist = (q0 - k0) + ri - ci
            distf = dist.astype(F32)
            if br == 0:
                expand = jnp.where(lax.shift_right_logical(k0 + bk, 6) == bj, 1.0, 0.0).astype(BF16)
                selk = jnp.dot(sel, expand, preferred_element_type=F32)
                mask = jnp.where(dist >= 0, selk, 0.0) > 0.5
            else:
                mask = jnp.where(dist >= 0, dist, NSA_WINDOW) < NSA_WINDOW
            ps, alphas = [], []
            for h in range(NSA_HEADS):
                j, e = divmod(h, 2)
                sc = (s_e if e == 0 else s_o)[j * tq:(j + 1) * tq] * (NSA_DH ** -0.5) - slopes[h] * distf
                p, alpha = _softmax_step(sc, mask, m_ref, l_ref, br * NSA_HEADS + h)
                ps.append(p.astype(BF16))
                alphas.append(alpha)
            pv = jnp.dot(jnp.concatenate(ps, axis=0), vd_ref[br, pl.ds(k0, tk), :], preferred_element_type=F32)
            for h in range(NSA_HEADS):
                i = br * NSA_HEADS + h
                acc_ref[i] = alphas[h] * acc_ref[i] + pv[h * tq:(h + 1) * tq]
            return carry
        return chunk

    n_chunks = (q0 + tq + tk - 1) // tk
    lax.fori_loop(0, n_chunks, make_chunk(0), 0)
    lax.fori_loop(jnp.maximum(q0 - (NSA_WINDOW - 1), 0) // tk, n_chunks, make_chunk(1), 0)

    gs = jax.nn.sigmoid(sm_ref[...])
    outs = []
    for j in range(4):
        pair = []
        for e in range(2):
            h = 2 * j + e
            c0 = MLA_D_ROPE + 3 * h
            o_s = acc_ref[h] / jnp.maximum(l_ref[h], TINY)
            o_w = acc_ref[NSA_HEADS + h] / jnp.maximum(l_ref[NSA_HEADS + h], TINY)
            pair.append(gs[:, c0:c0 + 1] * oc[h * tq:(h + 1) * tq] + gs[:, c0 + 1:c0 + 2] * o_s
                        + gs[:, c0 + 2:c0 + 3] * o_w)
        outs.append(jnp.where(lane < NSA_DH, pair[0], pair[1]))
    o_ref[...] = _rms(jnp.concatenate(outs, axis=-1), g_ref[...])


def _nsa_prompt(u, g, *, b, t, tq=128, tk=256):
    nq = t // tq
    return pl.pallas_call(
        functools.partial(_nsa_prompt_body, t=t, tq=tq, tk=tk),
        grid=(b, nq),
        in_specs=[
            pl.BlockSpec((tq, 512), lambda bi, qi: (bi * nq + qi, COL_NQ // 512)),
            pl.BlockSpec((tq, 128), lambda bi, qi: (bi * nq + qi, COL_SMALL // 128)),
            pl.BlockSpec((t, 384), lambda bi, qi: (bi, COL_NKV // 384)),
            pl.BlockSpec((1, GROUP_W), lambda bi, qi: (0, 0)),
        ],
        out_specs=pl.BlockSpec((tq, GROUP_W), lambda bi, qi: (bi * nq + qi, 0)),
        out_shape=jax.ShapeDtypeStruct((b * t, GROUP_W), F32),
        scratch_shapes=[pltpu.VMEM((256, 128), BF16), pltpu.VMEM((128, 128), BF16),
                        pltpu.VMEM((2, t, 128), BF16), pltpu.VMEM((2, t, 128), BF16), pltpu.VMEM((2, t, 128), BF16),
                        pltpu.VMEM((2 * NSA_HEADS, tq, 1), F32), pltpu.VMEM((2 * NSA_HEADS, tq, 1), F32),
                        pltpu.VMEM((2 * NSA_HEADS, tq, 128), F32)],
        compiler_params=pltpu.CompilerParams(
            dimension_semantics=("parallel", "arbitrary"), vmem_limit_bytes=VMEM_LIMIT),
        name="nsa_prompt",
    )(u, u, u, g.reshape(1, GROUP_W))


TS_PAD = 8
PAGES_PER_STEP = 16


def _page_specs(layer, width, pps):
    def spec(j):
        return pl.BlockSpec((None, None, 128, width), lambda b, g, pt: (layer, pt[b, g * pps + j], 0, 0))
    return [spec(j) for j in range(pps)]


def _pad_keys(x):
    return jnp.concatenate([x, jnp.zeros((128 - TS_PAD, x.shape[1]), x.dtype)], axis=0)


def _diff_decode_body(pt_ref, lv_ref, q_ref, new_ref, g_ref, *rest, pps, ts, past_len, lam_init):
    pages = rest[:pps]
    o_ref, qst_ref, m_ref, l_ref, acc_ref = rest[pps:]
    g = pl.program_id(1)
    rows = 2 * DIFF_HEADS * TS_PAD
    w = pps * 128

    @pl.when(g == 0)
    def _():
        qst_ref[...] = _diff_streams(q_ref[...], TS_PAD)
        m_ref[...] = jnp.full_like(m_ref, NEG_INF)
        l_ref[...] = jnp.zeros_like(l_ref)
        acc_ref[...] = jnp.zeros_like(acc_ref)

    qst = qst_ref[...]
    ri = lax.broadcasted_iota(jnp.int32, (rows, 1), 0)
    head = (ri >> 3) & (DIFF_HEADS - 1)
    sl = [2.0 ** (-8.0 * (h + 1) / DIFF_HEADS) for h in range(DIFF_HEADS)]
    slope = jnp.where(head == 0, sl[0], jnp.where(head == 1, sl[1], jnp.where(head == 2, sl[2], sl[3])))
    tok = ri & (TS_PAD - 1)

    def update(sc, mask, vals):
        if mask is not None:
            sc = jnp.where(mask, sc, NEG_INF)
        m_old = m_ref[...]
        m_new = jnp.maximum(m_old, jnp.max(sc, axis=-1, keepdims=True))
        p = jnp.exp(sc - m_new)
        if mask is not None:
            p = jnp.where(mask, p, 0.0)
        alpha = jnp.exp(m_old - m_new)
        l_ref[...] = alpha * l_ref[...] + jnp.sum(p, axis=-1, keepdims=True)
        m_ref[...] = m_new
        pb = p.astype(BF16)
        pv = jnp.dot(pb[:, 0:128], vals[0], preferred_element_type=F32)
        for j in range(1, len(vals)):
            pv += jnp.dot(pb[:, j * 128:(j + 1) * 128], vals[j], preferred_element_type=F32)
        acc_ref[...] = alpha * acc_ref[...] + pv

    s = jnp.concatenate([_dot_nt(qst, pages[j][:, 0:128].astype(BF16)) for j in range(pps)], axis=1)
    ci = lax.broadcasted_iota(jnp.int32, (rows, w), 1)
    distf = ((past_len - g * w) + tok - ci).astype(F32)
    update(s * (DIFF_DH ** -0.5) - slope * distf, None, [pages[j][:, 128:256].astype(BF16) for j in range(pps)])

    @pl.when(g == pl.num_programs(1) - 1)
    def _():
        new = _pad_keys(new_ref[...])
        sn = _dot_nt(qst, new[:, 0:128].astype(BF16))
        cn = lax.broadcasted_iota(jnp.int32, (rows, 128), 1)
        dist = tok - cn
        mask = (dist >= 0) & (cn < ts)
        update(sn * (DIFF_DH ** -0.5) - slope * dist.astype(F32), mask, [new[:, 128:256].astype(BF16)])
        lam = _diff_lambda(lv_ref, lam_init)
        acc = acc_ref[...] / jnp.maximum(l_ref[...], TINY)
        for h in range(DIFF_HEADS):
            o0 = acc[h * TS_PAD:(h + 1) * TS_PAD]
            o1 = acc[(DIFF_HEADS + h) * TS_PAD:(DIFF_HEADS + h + 1) * TS_PAD]
            o_ref[:, h * 128:(h + 1) * 128] = _rms(o0 - lam * o1, g_ref[...]) * (1.0 - lam_init)


def _diff_decode(page_table, u_s, cache, lv, subln_g, *, layer, ts, lam_init, pps=PAGES_PER_STEP):
    bs, n_pages = page_table.shape
    rows = 2 * DIFF_HEADS * TS_PAD
    grid_spec = pltpu.PrefetchScalarGridSpec(
        num_scalar_prefetch=1,
        grid=(bs, n_pages // pps),
        in_specs=[
            pl.BlockSpec((4, DIFF_DH), lambda b, g, pt: (0, 0)),
            pl.BlockSpec((None, TS_PAD, 512), lambda b, g, pt: (b, 0, COL_DQ // 512)),
            pl.BlockSpec((None, TS_PAD, 256), lambda b, g, pt: (b, 0, COL_DK // 256)),
            pl.BlockSpec((1, DIFF_DV), lambda b, g, pt: (0, 0)),
        ] + _page_specs(layer, 256, pps),
        out_specs=pl.BlockSpec((None, TS_PAD, GROUP_W), lambda b, g, pt: (b, 0, 0)),
        scratch_shapes=[pltpu.VMEM((rows, 128), BF16), pltpu.VMEM((rows, 1), F32), pltpu.VMEM((rows, 1), F32),
                        pltpu.VMEM((rows, 128), F32)],
    )
    return pl.pallas_call(
        functools.partial(_diff_decode_body, pps=pps, ts=ts, past_len=n_pages * 128, lam_init=lam_init),
        grid_spec=grid_spec,
        out_shape=jax.ShapeDtypeStruct((bs, TS_PAD, GROUP_W), F32),
        compiler_params=pltpu.CompilerParams(
            dimension_semantics=("parallel", "arbitrary"), vmem_limit_bytes=VMEM_LIMIT),
        name="diff_decode",
    )(page_table, lv, u_s, u_s, subln_g.reshape(1, DIFF_DV), *([cache] * pps))


def _mla_decode_body(pt_ref, q_ref, new_ref, wuv_ref, g_ref, *rest, pps, ts):
    pages = rest[:pps]
    o_ref, qst_ref, m_ref, l_ref, acc_ref = rest[pps:]
    g = pl.program_id(1)
    rows = MLA_HEADS * TS_PAD
    scale = (MLA_D_NOPE + MLA_D_ROPE) ** -0.5

    @pl.when(g == 0)
    def _():
        qst_ref[...] = jnp.concatenate([q_ref[:, h * 256:(h + 1) * 256] for h in range(MLA_HEADS)], axis=0)
        m_ref[...] = jnp.full_like(m_ref, NEG_INF)
        l_ref[...] = jnp.zeros_like(l_ref)
        acc_ref[...] = jnp.zeros_like(acc_ref)

    q_lat = qst_ref[:, 0:MLA_KV_RANK]
    q_rope = qst_ref[:, MLA_KV_RANK:MLA_KV_RANK + MLA_D_ROPE]

    def scores(lat, kr):
        return (_dot_nt(q_lat, lat) + _dot_nt(q_rope, kr)) * scale

    def update(sc, mask, vals):
        if mask is not None:
            sc = jnp.where(mask, sc, NEG_INF)
        m_old = m_ref[...]
        m_new = jnp.maximum(m_old, jnp.max(sc, axis=-1, keepdims=True))
        p = jnp.exp(sc - m_new)
        if mask is not None:
            p = jnp.where(mask, p, 0.0)
        alpha = jnp.exp(m_old - m_new)
        l_ref[...] = alpha * l_ref[...] + jnp.sum(p, axis=-1, keepdims=True)
        m_ref[...] = m_new
        pb = p.astype(BF16)
        pv = jnp.dot(pb[:, 0:128], vals[0], preferred_element_type=F32)
        for j in range(1, len(vals)):
            pv += jnp.dot(pb[:, j * 128:(j + 1) * 128], vals[j], preferred_element_type=F32)
        acc_ref[...] = alpha * acc_ref[...] + pv

    lats = [pages[j][:, 0:MLA_KV_RANK].astype(BF16) for j in range(pps)]
    s = jnp.concatenate(
        [scores(lats[j], pages[j][:, MLA_KV_RANK:MLA_KV_RANK + MLA_D_ROPE].astype(BF16)) for j in range(pps)], axis=1)
    update(s, None, lats)

    @pl.when(g == pl.num_programs(1) - 1)
    def _():
        new = _pad_keys(new_ref[...])
        lat = new[:, 0:MLA_KV_RANK].astype(BF16)
        sn = scores(lat, new[:, MLA_KV_RANK:MLA_KV_RANK + MLA_D_ROPE].astype(BF16))
        tok = lax.broadcasted_iota(jnp.int32, (rows, 128), 0) & (TS_PAD - 1)
        cn = lax.broadcasted_iota(jnp.int32, (rows, 128), 1)
        update(sn, (cn <= tok) & (cn < ts), [lat])
        acc = acc_ref[...] / jnp.maximum(l_ref[...], TINY)
        outs = [jnp.dot(acc[h * TS_PAD:(h + 1) * TS_PAD].astype(BF16), wuv_ref[h], preferred_element_type=F32)
                for h in range(MLA_HEADS)]
        o_ref[...] = _rms(jnp.concatenate(outs, axis=-1), g_ref[...])


def _mla_decode(page_table, q_s, k_s, cache, wuv, g, *, layer, ts, pps=PAGES_PER_STEP):
    bs, n_pages = page_table.shape
    rows = MLA_HEADS * TS_PAD
    width = cache.shape[-1]
    grid_spec = pltpu.PrefetchScalarGridSpec(
        num_scalar_prefetch=1,
        grid=(bs, n_pages // pps),
        in_specs=[
            pl.BlockSpec((None, TS_PAD, MLA_HEADS * 256), lambda b, g, pt: (b, 0, 0)),
            pl.BlockSpec((None, TS_PAD, 256), lambda b, g, pt: (b, 0, 0)),
            pl.BlockSpec((MLA_HEADS, 128, 128), lambda b, g, pt: (0, 0, 0)),
            pl.BlockSpec((1, GROUP_W), lambda b, g, pt: (0, 0)),
        ] + _page_specs(layer, width, pps),
        out_specs=pl.BlockSpec((None, TS_PAD, GROUP_W), lambda b, g, pt: (b, 0, 0)),
        scratch_shapes=[pltpu.VMEM((rows, 256), BF16), pltpu.VMEM((rows, 1), F32), pltpu.VMEM((rows, 1), F32),
                        pltpu.VMEM((rows, 128), F32)],
    )
    return pl.pallas_call(
        functools.partial(_mla_decode_body, pps=pps, ts=ts),
        grid_spec=grid_spec,
        out_shape=jax.ShapeDtypeStruct((bs, TS_PAD, GROUP_W), F32),
        compiler_params=pltpu.CompilerParams(
            dimension_semantics=("parallel", "arbitrary"), vmem_limit_bytes=VMEM_LIMIT),
        name="mla_decode",
    )(page_table, q_s, k_s, wuv, g.reshape(1, GROUP_W), *([cache] * pps))


def _rms_j(x, g):
    x32 = x.astype(F32)
    return x32 * lax.rsqrt(jnp.mean(x32 * x32, axis=-1, keepdims=True) + EPS) * g.astype(F32)


def _masked_softmax(s, mask):
    s = jnp.where(mask, s.astype(F32), NEG_INF)
    e = jnp.exp(s - jnp.max(s, axis=-1, keepdims=True)) * mask
    return e / jnp.maximum(jnp.sum(e, axis=-1, keepdims=True), TINY)


def _alibi(n):
    return 2.0 ** (-8.0 * jnp.arange(1, n + 1, dtype=F32) / n)


def _rope(x, pos):
    half = x.shape[-1] // 2
    inv = ROPE_THETA ** (-jnp.arange(half, dtype=F32) / half)
    ang = pos.astype(F32)[:, None] * inv
    ang = ang.reshape(ang.shape[0], *([1] * (x.ndim - 3)), half)
    cos, sin = jnp.cos(ang), jnp.sin(ang)
    x1, x2 = x[..., :half], x[..., half:]
    return jnp.concatenate([x1 * cos - x2 * sin, x1 * sin + x2 * cos], axis=-1)


def _over_q(fn, q_pos, *q_args):
    T = q_pos.shape[0]
    blk = Q_BLOCK if T % Q_BLOCK == 0 else T
    nb = T // blk

    def split(a):
        return jnp.moveaxis(a.reshape(a.shape[0], nb, blk, *a.shape[2:]), 1, 0)

    out = lax.map(lambda args: fn(*args), (q_pos.reshape(nb, blk),) + tuple(split(a) for a in q_args))
    out = jnp.moveaxis(out, 0, 1)
    return out.reshape(out.shape[0], T, *out.shape[3:])


def _diff_attention(q, k, v, q_pos, k_pos, lam, lam_init, subln_g):
    slopes = _alibi(DIFF_HEADS)[:, None, None]

    def block(pb, qb):
        s = jnp.einsum('bqhcd,bkcd->bchqk', qb, k) * DIFF_DH ** -0.5
        dist = pb[:, None] - k_pos[None, :]
        s = s - slopes * dist.astype(F32)
        p = _masked_softmax(s, dist >= 0)
        w = p[:, 0] - lam * p[:, 1]
        return jnp.einsum('bhqk,bkd->bqhd', w, v)

    o = _over_q(block, q_pos, q)
    o = _rms_j(o, subln_g) * (1.0 - lam_init)
    return o.reshape(o.shape[0], o.shape[1], -1)


def _mla_attention(q_lat, q_rope, ckv, krope, q_pos, k_pos):
    scale = (MLA_D_NOPE + MLA_D_ROPE) ** -0.5

    def block(pb, qlb, qrb):
        s = (jnp.einsum('bqhr,bkr->bhqk', qlb, ckv) + jnp.einsum('bqhd,bkd->bhqk', qrb, krope)) * scale
        p = _masked_softmax(s, k_pos[None, :] <= pb[:, None])
        return jnp.einsum('bhqk,bkr->bqhr', p, ckv)

    return _over_q(block, q_pos, q_lat, q_rope)


def _nsa_attention(q, gates, kc_seq, vc_seq, ks_seq, vs_seq, kw_seq, vw_seq, q_pos, w_start):
    B = q.shape[0]
    Tk = kc_seq.shape[1]
    nblk = -(-Tk // NSA_BLOCK)
    pad = nblk * NSA_BLOCK - Tk

    def blocks(a):
        return jnp.pad(a, ((0, 0), (0, pad), (0, 0))).reshape(B, nblk, NSA_BLOCK, NSA_DH)

    kc = blocks(kc_seq).mean(axis=2)
    vc = blocks(vc_seq).mean(axis=2)
    ks, vs = blocks(ks_seq), blocks(vs_seq)
    kw = jnp.pad(kw_seq, ((0, 0), (NSA_WINDOW, 0), (0, 0)))
    vw = jnp.pad(vw_seq, ((0, 0), (NSA_WINDOW, 0), (0, 0)))
    blk_ids = jnp.arange(nblk)
    blk_end = (blk_ids + 1) * NSA_BLOCK - 1
    offs = jnp.arange(NSA_BLOCK)
    n_sel = min(NSA_TOPK, nblk)
    slopes = _alibi(NSA_HEADS)[:, None, None]
    scale = NSA_DH ** -0.5

    def block(pb, qb, gb):
        nq = pb.shape[0]
        dist_c = pb[:, None] - blk_end[None, :]
        s_c = jnp.einsum('bqhd,bnd->bhqn', qb, kc) * scale - slopes * dist_c
        p_c = _masked_softmax(s_c, dist_c >= 0)
        o_c = jnp.einsum('bhqn,bnd->bqhd', p_c, vc)
        cur = pb // NSA_BLOCK
        imp = p_c.sum(axis=1)
        forced = (blk_ids[None] == 0) | (blk_ids[None] == cur[:, None]) | (blk_ids[None] == cur[:, None] - 1)
        valid = blk_ids[None] <= cur[:, None]
        score = jnp.where(valid, jnp.where(forced, FORCE_SCORE, imp), -1.0)
        top_val, sel = lax.top_k(score, n_sel)
        ks_sel = jax.vmap(lambda a, i: a[i])(ks, sel).reshape(B, nq, n_sel * NSA_BLOCK, NSA_DH)
        vs_sel = jax.vmap(lambda a, i: a[i])(vs, sel).reshape(B, nq, n_sel * NSA_BLOCK, NSA_DH)
        sel_pos = sel[..., None] * NSA_BLOCK + offs
        dist_s = (pb[None, :, None, None] - sel_pos).reshape(B, nq, n_sel * NSA_BLOCK)
        mask_s = ((dist_s >= 0) & jnp.repeat(top_val >= 0, NSA_BLOCK, axis=-1))[:, None]
        s_s = jnp.einsum('bqhd,bqkd->bhqk', qb, ks_sel) * scale - slopes * dist_s[:, None]
        p_s = _masked_softmax(s_s, mask_s)
        o_s = jnp.einsum('bhqk,bqkd->bqhd', p_s, vs_sel)
        r0 = pb[0] - w_start
        kwb = lax.dynamic_slice_in_dim(kw, r0, NSA_WINDOW + nq, axis=1)
        vwb = lax.dynamic_slice_in_dim(vw, r0, NSA_WINDOW + nq, axis=1)
        w_pos = pb[0] - NSA_WINDOW + jnp.arange(NSA_WINDOW + nq)
        dist_w = pb[:, None] - w_pos[None, :]
        mask_w = (dist_w >= 0) & (dist_w < NSA_WINDOW) & (w_pos[None, :] >= w_start)
        s_w = jnp.einsum('bqhd,bkd->bhqk', qb, kwb) * scale - slopes * dist_w
        p_w = _masked_softmax(s_w, mask_w)
        o_w = jnp.einsum('bhqk,bkd->bqhd', p_w, vwb)
        g = jax.nn.sigmoid(gb.astype(F32))
        return g[..., 0:1] * o_c + g[..., 1:2] * o_s + g[..., 2:3] * o_w

    return _over_q(block, q_pos, q, gates)


def _ssd_scan(x, dt, A, Bm, Cm, h0):
    B, T, H, P = x.shape
    Q = SSD_CHUNK if T % SSD_CHUNK == 0 else T
    nc = T // Q

    def chunks(a):
        return jnp.moveaxis(a.reshape(B, nc, Q, *a.shape[2:]), 1, 0)

    causal = jnp.tril(jnp.ones((Q, Q), bool))[None, :, :, None]

    def step(h, inp):
        xc, dtc, Bc, Cc = inp
        acum = jnp.cumsum(dtc * A, axis=1)
        seg = acum[:, :, None, :] - acum[:, None, :, :]
        decay = jnp.where(causal, jnp.exp(jnp.where(causal, seg, 0.0)), 0.0)
        w = jnp.einsum('bihn,bjhn->bijh', Cc, Bc) * decay * dtc[:, None, :, :]
        y = (jnp.einsum('bijh,bjhp->bihp', w, xc)
             + jnp.einsum('bihn,bhpn->bihp', Cc, h) * jnp.exp(acum)[..., None])
        to_end = jnp.exp(acum[:, -1:, :] - acum) * dtc
        h = h * jnp.exp(acum[:, -1])[:, :, None, None] + jnp.einsum('bjh,bjhn,bjhp->bhpn', to_end, Bc, xc)
        return h, y

    h, ys = lax.scan(step, h0, (chunks(x), chunks(dt), chunks(Bm), chunks(Cm)))
    return jnp.moveaxis(ys, 0, 1).reshape(B, T, H, P), h


def _ssd_mixer(z, xbc, dt_raw, conv_state, h0, conv_w, conv_b, dt_bias, a_log, d_skip, norm_g):
    B, T, _ = xbc.shape
    xpad = jnp.concatenate([conv_state, xbc], axis=1)
    new_conv = xpad[:, -(SSD_CONV - 1):]
    conv = lax.conv_general_dilated(xpad, conv_w[:, None, :], window_strides=(1,),
                                    padding='VALID', dimension_numbers=('NWC', 'WIO', 'NWC'),
                                    feature_group_count=SSD_CONV_CH) + conv_b
    xbc = jax.nn.silu(conv)
    xs, Bm, Cm = jnp.split(xbc, [GROUP_W, GROUP_W + SSD_GROUPS * SSD_STATE], axis=-1)
    xs = xs.reshape(B, T, SSD_HEADS, SSD_HEADDIM)
    rep = SSD_HEADS // SSD_GROUPS
    Bm = jnp.repeat(Bm.reshape(B, T, SSD_GROUPS, SSD_STATE), rep, axis=2)
    Cm = jnp.repeat(Cm.reshape(B, T, SSD_GROUPS, SSD_STATE), rep, axis=2)
    dt = jax.nn.softplus(dt_raw + dt_bias)
    A = -jnp.exp(a_log)
    y, h = _ssd_scan(xs, dt, A, Bm, Cm, h0)
    y = y + d_skip[:, None] * xs
    y = y.reshape(B, T, GROUP_W) * jax.nn.silu(z)
    return _rms_j(y, norm_g), new_conv, h


def _mixers(u, l, p, past, past_len, w_buf):
    B, T, _ = u.shape
    pos = past_len + jnp.arange(T)
    dq = u[..., COL_DQ:COL_DQ + 512]
    nq = u[..., COL_NQ:COL_NQ + 512]
    sz = u[..., COL_SZ:COL_SZ + 512]
    new_diff = u[..., COL_DK:COL_DK + 256]
    mckv = u[..., COL_MCKV:COL_MCKV + 128]
    mkr = u[..., COL_SMALL:COL_SMALL + 32]
    ngate = u[..., COL_SMALL + 32:COL_SMALL + 56]
    sdt = u[..., COL_SMALL + 56:COL_SMALL + 64]
    sxbc = u[..., COL_SXBC:COL_SXBC + 1024]
    mcq = u[..., COL_MCQ:COL_MCQ + 384]
    nkv = u[..., COL_NKV:COL_NKV + 384]

    new_mla = jnp.concatenate([_rms_j(mckv, p['mla_kv_norm_g']), _rope(mkr, pos)], axis=-1)
    new_nsa = nkv[..., :4 * NSA_DH]
    new_win_rows = nkv[..., 4 * NSA_DH:]

    if past is None:
        full_diff, full_mla, full_nsa, win_seq = new_diff, new_mla, new_nsa, new_win_rows
        w_start = 0
        conv_state = jnp.zeros((B, SSD_CONV - 1, SSD_CONV_CH), F32)
        h0 = jnp.zeros((B, SSD_HEADS, SSD_HEADDIM, SSD_STATE), F32)
    else:
        full_diff = jnp.concatenate([past['diff'], new_diff], axis=1)
        full_mla = jnp.concatenate([past['mla'], new_mla], axis=1)
        full_nsa = jnp.concatenate([past['nsa'], new_nsa], axis=1)
        win_seq = jnp.concatenate([past['win'], new_win_rows], axis=1)
        w_start = past_len - past['win'].shape[1]
        conv_state, h0 = past['conv'], past['h']
    Tk = full_diff.shape[1]
    k_pos = jnp.arange(Tk)
    new_win = jnp.pad(win_seq, ((0, 0), (max(0, w_buf - win_seq.shape[1]), 0), (0, 0)))[:, -w_buf:]

    lv = p['diff_lambda']
    lam_init = 0.8 - 0.6 * math.exp(-0.3 * l)
    lam = jnp.exp(jnp.sum(lv[0] * lv[1])) - jnp.exp(jnp.sum(lv[2] * lv[3])) + lam_init
    o_a = _diff_attention(dq.reshape(B, T, DIFF_HEADS, 2, DIFF_DH),
                          full_diff[..., :2 * DIFF_DH].reshape(B, Tk, 2, DIFF_DH),
                          full_diff[..., 2 * DIFF_DH:], pos, k_pos, lam, lam_init, p['diff_subln_g'])

    qm = (_rms_j(mcq, p['mla_q_norm_g']) @ p['mla_w_uq']).reshape(B, T, MLA_HEADS, MLA_D_NOPE + MLA_D_ROPE)
    q_nope, q_rope = qm[..., :MLA_D_NOPE], _rope(qm[..., MLA_D_NOPE:], pos)
    q_lat = jnp.einsum('bqhn,rhn->bqhr', q_nope, p['mla_w_uk'])
    o_lat = _mla_attention(q_lat, q_rope, full_mla[..., :MLA_KV_RANK], full_mla[..., MLA_KV_RANK:], pos, k_pos)
    o_b = jnp.einsum('bqhr,rhd->bqhd', o_lat, p['mla_w_uv']).reshape(B, T, GROUP_W)
    o_b = _rms_j(o_b, p['mla_out_g'])

    o_c = _nsa_attention(nq.reshape(B, T, NSA_HEADS, NSA_DH), ngate.reshape(B, T, NSA_HEADS, 3),
                         full_nsa[..., :NSA_DH], full_nsa[..., NSA_DH:2 * NSA_DH],
                         full_nsa[..., 2 * NSA_DH:3 * NSA_DH], full_nsa[..., 3 * NSA_DH:],
                         win_seq[..., :NSA_DH], win_seq[..., NSA_DH:], pos, w_start)
    o_c = _rms_j(o_c.reshape(B, T, GROUP_W), p['nsa_out_g'])

    o_d, new_conv, new_h = _ssd_mixer(sz, sxbc, sdt, conv_state, h0, p['ssd_conv_w'], p['ssd_conv_b'],
                                      p['ssd_dt_bias'], p['ssd_a_log'], p['ssd_d'], p['ssd_norm_g'])
    return (o_a, o_b, o_c, o_d), (new_diff, new_mla, new_nsa, new_win, new_conv, new_h)


def _permute_w_in(w_in):
    def sl(a, b):
        return w_in[..., a:b]
    z64 = jnp.zeros(w_in.shape[:-1] + (64,), w_in.dtype)
    return jnp.concatenate([
        sl(0, 512), sl(1312, 1824), sl(2232, 2744), sl(512, 640), sl(640, 768), sl(1152, 1280),
        sl(1280, 1312), sl(2208, 2232), sl(3768, 3776), z64,
        sl(2744, 3768), sl(768, 1152), sl(1824, 2208)], axis=-1)


def kernel(x_prompt, x_sample, cache_diff_kv, cache_mla, cache_nsa_kv, cache_nsa_win, state_ssd_conv, state_ssd_h, page_table, ffn1_pre_g, ffn1_post_g, ffn1_w1, ffn1_w2, mix_pre_g, mix_post_g, w_in, w_out, diff_lambda, diff_subln_g, mla_q_norm_g, mla_w_uq, mla_kv_norm_g, mla_w_uk, mla_w_uv, mla_out_g, nsa_out_g, ssd_conv_w, ssd_conv_b, ssd_dt_bias, ssd_a_log, ssd_d, ssd_norm_g, ffn2_pre_g, ffn2_post_g, ffn2_w1, ffn2_w2):
    depth = w_in.shape[0]
    bp, tp, d = x_prompt.shape
    bs, ts, _ = x_sample.shape
    mp, ms = bp * tp, bs * ts
    past_len = page_table.shape[1] * cache_diff_kv.shape[2]
    w_buf = cache_nsa_win.shape[2]

    w_in_p = _permute_w_in(w_in).astype(BF16)
    w_out_b = w_out.astype(BF16).reshape(depth, 4, GROUP_W, d)
    f1w1, f1w2 = ffn1_w1.astype(BF16), ffn1_w2.astype(BF16)
    f2w1, f2w2 = ffn2_w1.astype(BF16), ffn2_w2.astype(BF16)

    mix_names = dict(diff_lambda=diff_lambda, diff_subln_g=diff_subln_g, mla_q_norm_g=mla_q_norm_g,
                     mla_w_uq=mla_w_uq, mla_kv_norm_g=mla_kv_norm_g, mla_w_uk=mla_w_uk, mla_w_uv=mla_w_uv,
                     mla_out_g=mla_out_g, nsa_out_g=nsa_out_g, ssd_conv_w=ssd_conv_w, ssd_conv_b=ssd_conv_b,
                     ssd_dt_bias=ssd_dt_bias, ssd_a_log=ssd_a_log, ssd_d=ssd_d, ssd_norm_g=ssd_norm_g)

    x = jnp.concatenate([x_prompt.reshape(mp, d), x_sample.reshape(ms, d)], axis=0)
    pos_all = jnp.concatenate([jnp.tile(jnp.arange(tp), bp), jnp.tile(past_len + jnp.arange(ts), bs)])
    cos_t, sin_t = _rope_tables(pos_all)
    st_p, st_s = [], []
    for l in range(depth):
        p = {k: v[l] for k, v in mix_names.items()}
        lam_init = 0.8 - 0.6 * math.exp(-0.3 * l)
        wuq, wuk, wuv = _mla_weight_layout(mla_w_uq[l], mla_w_uk[l], mla_w_uv[l])
        x = _ffn(x, ffn1_pre_g[l], ffn1_post_g[l], f1w1[l], f1w2[l])
        u = _in_proj(x, mix_pre_g[l], w_in_p[l])
        q_mla, k_mla = _mla_prep(u, cos_t, sin_t, mla_q_norm_g[l], mla_kv_norm_g[l], wuq, wuk)
        oa_p = _diff_prompt(u, diff_lambda[l], diff_subln_g[l], b=bp, t=tp, lam_init=lam_init)
        ob_p = _mla_prompt(q_mla, k_mla, wuv, mla_out_g[l], b=bp, t=tp)
        oc_p = _nsa_prompt(u, nsa_out_g[l], b=bp, t=tp)
        u_p = u[:mp].reshape(bp, tp, -1)
        od_p, conv_p, h_p = _ssd_mixer(
            u_p[..., COL_SZ:COL_SZ + 512], u_p[..., COL_SXBC:COL_SXBC + 1024],
            u_p[..., COL_SMALL + 56:COL_SMALL + 64], jnp.zeros((bp, SSD_CONV - 1, SSD_CONV_CH), F32),
            jnp.zeros((bp, SSD_HEADS, SSD_HEADDIM, SSD_STATE), F32), p['ssd_conv_w'], p['ssd_conv_b'],
            p['ssd_dt_bias'], p['ssd_a_log'], p['ssd_d'], p['ssd_norm_g'])
        new_p = (u_p[..., COL_DK:COL_DK + 256], k_mla[:mp, :160].reshape(bp, tp, 160),
                 u_p[..., COL_NKV:COL_NKV + 256], u_p[:, tp - w_buf:, COL_NKV + 256:COL_NKV + 384], conv_p, h_p)
        o_p = (oa_p, ob_p, oc_p, od_p)
        u_s = u[mp:].reshape(bs, ts, -1)
        pad_ts = lambda a: jnp.pad(a, ((0, 0), (0, TS_PAD - ts), (0, 0)))
        u_s8 = pad_ts(u_s)
        oa_s = _diff_decode(page_table, u_s8, cache_diff_kv, diff_lambda[l], diff_subln_g[l],
                            layer=l, ts=ts, lam_init=lam_init)[:, :ts]
        ob_s = _mla_decode(page_table, pad_ts(q_mla[mp:].reshape(bs, ts, -1)), pad_ts(k_mla[mp:].reshape(bs, ts, -1)),
                           cache_mla, wuv, mla_out_g[l], layer=l, ts=ts)[:, :ts]
        nkv_s = u_s[..., COL_NKV:COL_NKV + 384]
        full_nsa = jnp.concatenate([cache_nsa_kv[l, page_table].reshape(bs, past_len, -1), nkv_s[..., :256]], axis=1)
        win_seq = jnp.concatenate([cache_nsa_win[l], nkv_s[..., 256:]], axis=1)
        oc_s = _nsa_attention(u_s[..., COL_NQ:COL_NQ + 512].reshape(bs, ts, NSA_HEADS, NSA_DH),
                              u_s[..., COL_SMALL + 32:COL_SMALL + 56].reshape(bs, ts, NSA_HEADS, 3),
                              full_nsa[..., :64], full_nsa[..., 64:128], full_nsa[..., 128:192], full_nsa[..., 192:],
                              win_seq[..., :64], win_seq[..., 64:], past_len + jnp.arange(ts), past_len - w_buf)
        oc_s = _rms_j(oc_s.reshape(bs, ts, GROUP_W), p['nsa_out_g'])
        od_s, conv_s, h_s = _ssd_mixer(
            u_s[..., COL_SZ:COL_SZ + 512], u_s[..., COL_SXBC:COL_SXBC + 1024],
            u_s[..., COL_SMALL + 56:COL_SMALL + 64], state_ssd_conv[l], state_ssd_h[l], p['ssd_conv_w'],
            p['ssd_conv_b'], p['ssd_dt_bias'], p['ssd_a_log'], p['ssd_d'], p['ssd_norm_g'])
        new_s = (u_s[..., COL_DK:COL_DK + 256], k_mla[mp:, :160].reshape(bs, ts, 160), nkv_s[..., :256],
                 win_seq[:, -w_buf:], conv_s, h_s)
        o_s = (oa_s, ob_s, oc_s, od_s)
        o = [jnp.concatenate([a.reshape(mp, GROUP_W), b.reshape(ms, GROUP_W)], axis=0)
             for a, b in zip(o_p, o_s)]
        x = _out_proj(x, o[0], o[1], o[2], o[3], w_out_b[l], mix_post_g[l])
        x = _ffn(x, ffn2_pre_g[l], ffn2_post_g[l], f2w1[l], f2w2[l])
        st_p.append(new_p)
        st_s.append(new_s)

    def stacked(states, i):
        return jnp.stack([s[i] for s in states])

    return (x[:mp].reshape(bp, tp, d), x[mp:].reshape(bs, ts, d),
            stacked(st_p, 0), stacked(st_s, 0),
            stacked(st_p, 1), stacked(st_s, 1),
            stacked(st_p, 2), stacked(st_s, 2),
            stacked(st_p, 3), stacked(st_s, 3),
            stacked(st_p, 4), stacked(st_s, 4),
            stacked(st_p, 5), stacked(st_s, 5))
```

```python
import functools
import math

import numpy as np
import jax
import jax.numpy as jnp
from jax import lax
from jax.experimental import pallas as pl
from jax.experimental.pallas import tpu as pltpu

D_MODEL = 2048
GROUP_W = 512
DIFF_HEADS = 4
DIFF_DH = 64
DIFF_DV = 128
MLA_HEADS = 4
MLA_DV = 128
MLA_D_NOPE = 128
MLA_D_ROPE = 32
MLA_Q_RANK = 384
MLA_KV_RANK = 128
ROPE_THETA = 10000.0
NSA_HEADS = 8
NSA_DH = 64
NSA_BLOCK = 64
NSA_TOPK = 16
NSA_WINDOW = 512
FORCE_SCORE = 1.0e4
SSD_HEADDIM = 64
SSD_HEADS = 8
SSD_STATE = 128
SSD_GROUPS = 2
SSD_CONV = 4
SSD_CHUNK = 128
SSD_CONV_CH = 1024
D_FF = 5632
Q_BLOCK = 128
EPS = 1e-6
NEG_INF = -1e30
TINY = 1e-30
F32 = jnp.float32
BF16 = jnp.bfloat16

COL_DQ = 0
COL_NQ = 512
COL_SZ = 1024
COL_DK = 1536
COL_DV = 1664
COL_MCKV = 1792
COL_SMALL = 1920
COL_SXBC = 2048
COL_MCQ = 3072
COL_NKV = 3456
N_IN_PAD = 3840

VMEM_LIMIT = 56 * 1024 * 1024


def _rms(x, g):
    return x * lax.rsqrt(jnp.mean(x * x, axis=-1, keepdims=True) + EPS) * g


def _ffn_body(x_ref, pre_ref, post_ref, wg_ref, wu_ref, w2_ref, o_ref, xn_ref, acc_ref):
    j = pl.program_id(1)

    @pl.when(j == 0)
    def _():
        xn_ref[...] = _rms(x_ref[...], pre_ref[...]).astype(BF16)
        acc_ref[...] = jnp.zeros_like(acc_ref)

    xn = xn_ref[...]
    g = jnp.dot(xn, wg_ref[...], preferred_element_type=F32)
    u = jnp.dot(xn, wu_ref[...], preferred_element_type=F32)
    h = (g * jax.nn.sigmoid(g) * u).astype(BF16)
    acc_ref[...] += jnp.dot(h, w2_ref[...], preferred_element_type=F32)

    @pl.when(j == pl.num_programs(1) - 1)
    def _():
        o_ref[...] = x_ref[...] + 0.5 * _rms(acc_ref[...], post_ref[...])


def _ffn(x, pre_g, post_g, w1, w2, *, tm=544, tf=512):
    m, d = x.shape
    nf = D_FF // tf
    return pl.pallas_call(
        _ffn_body,
        grid=(m // tm, nf),
        in_specs=[
            pl.BlockSpec((tm, d), lambda i, j: (i, 0)),
            pl.BlockSpec((1, d), lambda i, j: (0, 0)),
            pl.BlockSpec((1, d), lambda i, j: (0, 0)),
            pl.BlockSpec((d, tf), lambda i, j: (0, j)),
            pl.BlockSpec((d, tf), lambda i, j: (0, j + nf)),
            pl.BlockSpec((tf, d), lambda i, j: (j, 0)),
        ],
        out_specs=pl.BlockSpec((tm, d), lambda i, j: (i, 0)),
        out_shape=jax.ShapeDtypeStruct((m, d), F32),
        scratch_shapes=[pltpu.VMEM((tm, d), BF16), pltpu.VMEM((tm, d), F32)],
        compiler_params=pltpu.CompilerParams(
            dimension_semantics=("parallel", "arbitrary"), vmem_limit_bytes=VMEM_LIMIT),
        name="ffn",
    )(x, pre_g.reshape(1, d), post_g.reshape(1, d), w1, w1, w2)


def _in_proj_body(x_ref, g_ref, w_ref, o_ref, xn_ref):
    @pl.when(pl.program_id(1) == 0)
    def _():
        xn_ref[...] = _rms(x_ref[...], g_ref[...]).astype(BF16)

    o_ref[...] = jnp.dot(xn_ref[...], w_ref[...], preferred_element_type=F32)


def _in_proj(x, g, w, *, tm=544, tn=1280):
    m, d = x.shape
    n = w.shape[1]
    return pl.pallas_call(
        _in_proj_body,
        grid=(m // tm, n // tn),
        in_specs=[
            pl.BlockSpec((tm, d), lambda i, j: (i, 0)),
            pl.BlockSpec((1, d), lambda i, j: (0, 0)),
            pl.BlockSpec((d, tn), lambda i, j: (0, j)),
        ],
        out_specs=pl.BlockSpec((tm, tn), lambda i, j: (i, j)),
        out_shape=jax.ShapeDtypeStruct((m, n), F32),
        scratch_shapes=[pltpu.VMEM((tm, d), BF16)],
        compiler_params=pltpu.CompilerParams(
            dimension_semantics=("parallel", "arbitrary"), vmem_limit_bytes=VMEM_LIMIT),
        name="in_proj",
    )(x, g.reshape(1, d), w)


def _out_proj_body(x_ref, a_ref, b_ref, c_ref, d_ref, w_ref, g_ref, o_ref):
    mix = jnp.dot(a_ref[...].astype(BF16), w_ref[0], preferred_element_type=F32)
    mix += jnp.dot(b_ref[...].astype(BF16), w_ref[1], preferred_element_type=F32)
    mix += jnp.dot(c_ref[...].astype(BF16), w_ref[2], preferred_element_type=F32)
    mix += jnp.dot(d_ref[...].astype(BF16), w_ref[3], preferred_element_type=F32)
    o_ref[...] = x_ref[...] + _rms(mix, g_ref[...])


def _out_proj(x, o_a, o_b, o_c, o_d, w, g, *, tm=272):
    m, d = x.shape
    tok = pl.BlockSpec((tm, GROUP_W), lambda i: (i, 0))
    return pl.pallas_call(
        _out_proj_body,
        grid=(m // tm,),
        in_specs=[
            pl.BlockSpec((tm, d), lambda i: (i, 0)),
            tok, tok, tok, tok,
            pl.BlockSpec((4, GROUP_W, d), lambda i: (0, 0, 0)),
            pl.BlockSpec((1, d), lambda i: (0, 0)),
        ],
        out_specs=pl.BlockSpec((tm, d), lambda i: (i, 0)),
        out_shape=jax.ShapeDtypeStruct((m, d), F32),
        compiler_params=pltpu.CompilerParams(
            dimension_semantics=("parallel",), vmem_limit_bytes=VMEM_LIMIT),
        name="out_proj",
    )(x, o_a, o_b, o_c, o_d, w, g.reshape(1, d))


def _softmax_step(sc, mask, m_ref, l_ref, idx):
    sc = jnp.where(mask, sc, NEG_INF)
    m_old = m_ref[idx]
    m_new = jnp.maximum(m_old, jnp.max(sc, axis=-1, keepdims=True))
    p = jnp.where(mask, jnp.exp(sc - m_new), 0.0)
    alpha = jnp.exp(m_old - m_new)
    l_ref[idx] = alpha * l_ref[idx] + jnp.sum(p, axis=-1, keepdims=True)
    m_ref[idx] = m_new
    return p, alpha


def _dot_nt(a, b):
    return lax.dot_general(a, b, (((1,), (1,)), ((), ())), preferred_element_type=F32)


def _diff_lambda(lv_ref, lam_init):
    lv = lv_ref[...]
    a = jnp.sum(lv[0:1] * lv[1:2], axis=-1, keepdims=True)
    b = jnp.sum(lv[2:3] * lv[3:4], axis=-1, keepdims=True)
    return jnp.exp(a) - jnp.exp(b) + lam_init


def _diff_streams(q, rows):
    lane = lax.broadcasted_iota(jnp.int32, (rows, 128), 1)
    parts = []
    for c in range(2):
        keep = (lane < DIFF_DH) if c == 0 else (lane >= DIFF_DH)
        for h in range(DIFF_HEADS):
            parts.append(jnp.where(keep, q[:, h * 128:(h + 1) * 128], 0.0))
    return jnp.concatenate(parts, axis=0).astype(BF16)


def _diff_prompt_body(lv_ref, q_ref, k_ref, v_ref, g_ref, o_ref, m_ref, l_ref, acc_ref, *, tq, tk, lam_init):
    qi = pl.program_id(1)
    q0 = qi * tq
    qst = _diff_streams(q_ref[...], tq)
    m_ref[...] = jnp.full_like(m_ref, NEG_INF)
    l_ref[...] = jnp.zeros_like(l_ref)
    acc_ref[...] = jnp.zeros_like(acc_ref)
    slopes = [2.0 ** (-8.0 * (h + 1) / DIFF_HEADS) for h in range(DIFF_HEADS)]
    ri = lax.broadcasted_iota(jnp.int32, (tq, tk), 0)
    ci = lax.broadcasted_iota(jnp.int32, (tq, tk), 1)

    def chunk(kc, carry):
        k0 = pl.multiple_of(kc * tk, tk)
        kb = k_ref[pl.ds(k0, tk), :].astype(BF16)
        vb = v_ref[pl.ds(k0, tk), :].astype(BF16)
        s_all = _dot_nt(qst, kb)
        dist = (q0 - k0) + ri - ci
        mask = dist >= 0
        distf = dist.astype(F32)
        ps, alphas = [], []
        for s in range(2 * DIFF_HEADS):
            sc = s_all[s * tq:(s + 1) * tq] * (DIFF_DH ** -0.5) - slopes[s % DIFF_HEADS] * distf
            p, alpha = _softmax_step(sc, mask, m_ref, l_ref, s)
            ps.append(p.astype(BF16))
            alphas.append(alpha)
        pv = jnp.dot(jnp.concatenate(ps, axis=0), vb, preferred_element_type=F32)
        for s in range(2 * DIFF_HEADS):
            acc_ref[s] = alphas[s] * acc_ref[s] + pv[s * tq:(s + 1) * tq]
        return carry

    lax.fori_loop(0, (q0 + tq + tk - 1) // tk, chunk, 0)
    lam = _diff_lambda(lv_ref, lam_init)
    for h in range(DIFF_HEADS):
        o0 = acc_ref[h] / jnp.maximum(l_ref[h], TINY)
        o1 = acc_ref[DIFF_HEADS + h] / jnp.maximum(l_ref[DIFF_HEADS + h], TINY)
        o_ref[:, h * 128:(h + 1) * 128] = _rms(o0 - lam * o1, g_ref[...]) * (1.0 - lam_init)


def _diff_prompt(u, lv, subln_g, *, b, t, lam_init, tq=128, tk=256):
    nq = t // tq
    return pl.pallas_call(
        functools.partial(_diff_prompt_body, tq=tq, tk=tk, lam_init=lam_init),
        grid=(b, nq),
        in_specs=[
            pl.BlockSpec((4, DIFF_DH), lambda bi, qi: (0, 0)),
            pl.BlockSpec((tq, 512), lambda bi, qi: (bi * nq + qi, COL_DQ // 512)),
            pl.BlockSpec((t, 128), lambda bi, qi: (bi, COL_DK // 128)),
            pl.BlockSpec((t, 128), lambda bi, qi: (bi, COL_DV // 128)),
            pl.BlockSpec((1, DIFF_DV), lambda bi, qi: (0, 0)),
        ],
        out_specs=pl.BlockSpec((tq, GROUP_W), lambda bi, qi: (bi * nq + qi, 0)),
        out_shape=jax.ShapeDtypeStruct((b * t, GROUP_W), F32),
        scratch_shapes=[pltpu.VMEM((8, tq, 1), F32), pltpu.VMEM((8, tq, 1), F32),
                        pltpu.VMEM((8, tq, 128), F32)],
        compiler_params=pltpu.CompilerParams(
            dimension_semantics=("parallel", "parallel"), vmem_limit_bytes=VMEM_LIMIT),
        name="diff_prompt",
    )(lv, u, u, u, subln_g.reshape(1, DIFF_DV))


def _rope_tables(pos):
    half = MLA_D_ROPE // 2
    inv = ROPE_THETA ** (-jnp.arange(half, dtype=F32) / half)
    ang = pos.astype(F32)[:, None] * inv
    cos, sin = jnp.cos(ang), jnp.sin(ang)
    cos_t = jnp.tile(cos, (1, 128 // half))
    sin_t = jnp.tile(jnp.concatenate([-sin, sin], axis=-1), (1, 128 // MLA_D_ROPE))
    return cos_t, sin_t


def _rope_lanes(x, cos_t, sin_t):
    lane = lax.broadcasted_iota(jnp.int32, x.shape, 1)
    first = (lane % MLA_D_ROPE) < (MLA_D_ROPE // 2)
    partner = jnp.where(first, pltpu.roll(x, 128 - MLA_D_ROPE // 2, 1), pltpu.roll(x, MLA_D_ROPE // 2, 1))
    return x * cos_t + partner * sin_t


def _mla_weight_layout(w_uq, w_uk, w_uv):
    w = w_uq.reshape(MLA_Q_RANK, MLA_HEADS, MLA_D_NOPE + MLA_D_ROPE)
    wuq = jnp.concatenate([w[:, :, :MLA_D_NOPE].reshape(MLA_Q_RANK, -1),
                           w[:, :, MLA_D_NOPE:].reshape(MLA_Q_RANK, -1)], axis=-1)
    return wuq.astype(BF16), w_uk.transpose(1, 2, 0).astype(BF16), w_uv.transpose(1, 0, 2).astype(BF16)


def _mla_prep_body(cq_ref, ckv_ref, sm_ref, cos_ref, sin_ref, gq_ref, gkv_ref, wuq_ref, wuk_ref, q_ref, k_ref):
    tm = cq_ref.shape[0]
    cos_t, sin_t = cos_ref[...], sin_ref[...]
    lane = lax.broadcasted_iota(jnp.int32, (tm, 128), 1)
    low = lane < MLA_D_ROPE
    k_ref[:, 0:128] = _rms(ckv_ref[...], gkv_ref[...])
    k_ref[:, 128:256] = jnp.where(low, _rope_lanes(sm_ref[...], cos_t, sin_t), 0.0)
    qm = jnp.dot(_rms(cq_ref[...], gq_ref[...]).astype(BF16), wuq_ref[...], preferred_element_type=F32)
    qr = _rope_lanes(qm[:, MLA_HEADS * MLA_D_NOPE:], cos_t, sin_t)
    for h in range(MLA_HEADS):
        qn = qm[:, h * MLA_D_NOPE:(h + 1) * MLA_D_NOPE].astype(BF16)
        q_ref[:, h * 256:h * 256 + 128] = jnp.dot(qn, wuk_ref[h], preferred_element_type=F32).astype(BF16)
        qr_h = qr if h == 0 else pltpu.roll(qr, 128 - h * MLA_D_ROPE, 1)
        q_ref[:, h * 256 + 128:(h + 1) * 256] = jnp.where(low, qr_h, 0.0).astype(BF16)


def _mla_prep(u, cos_t, sin_t, gq, gkv, wuq, wuk, *, tm=544):
    m = u.shape[0]
    row = lambda i: (i, 0)
    fixed2 = lambda i: (0, 0)
    return pl.pallas_call(
        _mla_prep_body,
        grid=(m // tm,),
        in_specs=[
            pl.BlockSpec((tm, MLA_Q_RANK), lambda i: (i, COL_MCQ // MLA_Q_RANK)),
            pl.BlockSpec((tm, 128), lambda i: (i, COL_MCKV // 128)),
            pl.BlockSpec((tm, 128), lambda i: (i, COL_SMALL // 128)),
            pl.BlockSpec((tm, 128), row),
            pl.BlockSpec((tm, 128), row),
            pl.BlockSpec((1, MLA_Q_RANK), fixed2),
            pl.BlockSpec((1, MLA_KV_RANK), fixed2),
            pl.BlockSpec((MLA_Q_RANK, 640), fixed2),
            pl.BlockSpec((MLA_HEADS, 128, 128), lambda i: (0, 0, 0)),
        ],
        out_specs=[pl.BlockSpec((tm, MLA_HEADS * 256), row), pl.BlockSpec((tm, 256), row)],
        out_shape=[jax.ShapeDtypeStruct((m, MLA_HEADS * 256), BF16), jax.ShapeDtypeStruct((m, 256), F32)],
        compiler_params=pltpu.CompilerParams(dimension_semantics=("parallel",), vmem_limit_bytes=VMEM_LIMIT),
        name="mla_prep",
    )(u, u, u, cos_t, sin_t, gq.reshape(1, -1), gkv.reshape(1, -1), wuq, wuk)


def _mla_prompt_body(q_ref, k_ref, wuv_ref, g_ref, o_ref, m_ref, l_ref, acc_ref, *, tq, tk):
    qi = pl.program_id(1)
    q0 = qi * tq
    qst = jnp.concatenate([q_ref[:, h * 256:(h + 1) * 256] for h in range(MLA_HEADS)], axis=0)
    m_ref[...] = jnp.full_like(m_ref, NEG_INF)
    l_ref[...] = jnp.zeros_like(l_ref)
    acc_ref[...] = jnp.zeros_like(acc_ref)
    ri = lax.broadcasted_iota(jnp.int32, (tq, tk), 0)
    ci = lax.broadcasted_iota(jnp.int32, (tq, tk), 1)
    scale = (MLA_D_NOPE + MLA_D_ROPE) ** -0.5

    def chunk(kc, carry):
        k0 = pl.multiple_of(kc * tk, tk)
        kb = k_ref[pl.ds(k0, tk), :].astype(BF16)
        s_all = _dot_nt(qst, kb)
        mask = ((q0 - k0) + ri - ci) >= 0
        ps, alphas = [], []
        for h in range(MLA_HEADS):
            p, alpha = _softmax_step(s_all[h * tq:(h + 1) * tq] * scale, mask, m_ref, l_ref, h)
            ps.append(p.astype(BF16))
            alphas.append(alpha)
        pv = jnp.dot(jnp.concatenate(ps, axis=0), kb[:, 0:MLA_KV_RANK], preferred_element_type=F32)
        for h in range(MLA_HEADS):
            acc_ref[h] = alphas[h] * acc_ref[h] + pv[h * tq:(h + 1) * tq]
        return carry

    lax.fori_loop(0, (q0 + tq + tk - 1) // tk, chunk, 0)
    outs = []
    for h in range(MLA_HEADS):
        o_lat = (acc_ref[h] / jnp.maximum(l_ref[h], TINY)).astype(BF16)
        outs.append(jnp.dot(o_lat, wuv_ref[h], preferred_element_type=F32))
    o_ref[...] = _rms(jnp.concatenate(outs, axis=-1), g_ref[...])


def _mla_prompt(q, k, wuv, g, *, b, t, tq=128, tk=256):
    nq = t // tq
    return pl.pallas_call(
        functools.partial(_mla_prompt_body, tq=tq, tk=tk),
        grid=(b, nq),
        in_specs=[
            pl.BlockSpec((tq, MLA_HEADS * 256), lambda bi, qi: (bi * nq + qi, 0)),
            pl.BlockSpec((t, 256), lambda bi, qi: (bi, 0)),
            pl.BlockSpec((MLA_HEADS, 128, 128), lambda bi, qi: (0, 0, 0)),
            pl.BlockSpec((1, GROUP_W), lambda bi, qi: (0, 0)),
        ],
        out_specs=pl.BlockSpec((tq, GROUP_W), lambda bi, qi: (bi * nq + qi, 0)),
        out_shape=jax.ShapeDtypeStruct((b * t, GROUP_W), F32),
        scratch_shapes=[pltpu.VMEM((MLA_HEADS, tq, 1), F32), pltpu.VMEM((MLA_HEADS, tq, 1), F32),
                        pltpu.VMEM((MLA_HEADS, tq, 128), F32)],
        compiler_params=pltpu.CompilerParams(
            dimension_semantics=("parallel", "parallel"), vmem_limit_bytes=VMEM_LIMIT),
        name="mla_prompt",
    )(q, k, wuv, g.reshape(1, GROUP_W))


def _nsa_slopes():
    return [2.0 ** (-8.0 * (h + 1) / NSA_HEADS) for h in range(NSA_HEADS)]


def _nsa_select(imp, posq, lane):
    cur = lax.shift_right_logical(posq, 6)
    forced = (lane == 0) | (lane == cur) | (lane == cur - 1)
    score = jnp.where(lane <= cur, jnp.where(forced, FORCE_SCORE, imp), -1.0)
    lane_f = lane.astype(F32)
    sel = jnp.zeros(imp.shape, F32)
    for _ in range(NSA_TOPK):
        mx = jnp.max(score, axis=-1, keepdims=True)
        idx = jnp.min(jnp.where(score == mx, lane_f, 1e9), axis=-1, keepdims=True)
        hit = lane_f == idx
        sel = jnp.where(hit & (mx >= 0.0), 1.0, sel)
        score = jnp.where(hit, -2.0, score)
    return sel


def _nsa_prompt_body(q_ref, sm_ref, kv_ref, g_ref, o_ref, kc_ref, vc_ref, ke_ref, ko_ref, vd_ref,
                     m_ref, l_ref, acc_ref, *, t, tq, tk):
    qi = pl.program_id(1)
    q0 = qi * tq
    nb = t // NSA_BLOCK
    slopes = _nsa_slopes()

    @pl.when(qi == 0)
    def _():
        lo_t = lax.broadcasted_iota(jnp.int32, (t, 128), 1) < NSA_DH
        for br in range(2):
            kvb = kv_ref[:, 128 * (br + 1):128 * (br + 2)]
            sw = pltpu.roll(kvb, NSA_DH, 1)
            ke_ref[br] = jnp.where(lo_t, kvb, 0.0).astype(BF16)
            ko_ref[br] = jnp.where(lo_t, 0.0, sw).astype(BF16)
            vd_ref[br] = jnp.where(lo_t, sw, kvb).astype(BF16)
        cm = jnp.sum(kv_ref[:, 0:128].reshape(nb, NSA_BLOCK, 128), axis=1) * (1.0 / NSA_BLOCK)
        cm = jnp.concatenate([cm, jnp.zeros((128 - nb, 128), F32)], axis=0)
        sw = pltpu.roll(cm, NSA_DH, 1)
        lo_c = lax.broadcasted_iota(jnp.int32, (128, 128), 1) < NSA_DH
        kc_ref[0:128, :] = jnp.where(lo_c, cm, 0.0).astype(BF16)
        kc_ref[128:256, :] = jnp.where(lo_c, 0.0, sw).astype(BF16)
        vc_ref[...] = jnp.where(lo_c, sw, cm).astype(BF16)

    q4 = jnp.concatenate([q_ref[:, j * 128:(j + 1) * 128] for j in range(4)], axis=0).astype(BF16)
    lane = lax.broadcasted_iota(jnp.int32, (tq, 128), 1)
    posq = q0 + lax.broadcasted_iota(jnp.int32, (tq, 128), 0)

    s_c = _dot_nt(q4, kc_ref[...])
    dist_c = posq - (NSA_BLOCK * (lane + 1) - 1)
    mask_c = dist_c >= 0
    dist_cf = dist_c.astype(F32)
    imp = jnp.zeros((tq, 128), F32)
    pcs = []
    for h in range(NSA_HEADS):
        j, e = divmod(h, 2)
        sc = s_c[j * tq:(j + 1) * tq, e * 128:(e + 1) * 128] * (NSA_DH ** -0.5) - slopes[h] * dist_cf
        sc = jnp.where(mask_c, sc, NEG_INF)
        ex = jnp.where(mask_c, jnp.exp(sc - jnp.max(sc, axis=-1, keepdims=True)), 0.0)
        pc = ex / jnp.maximum(jnp.sum(ex, axis=-1, keepdims=True), TINY)
        imp = imp + pc
        pcs.append(pc.astype(BF16))
    oc = jnp.dot(jnp.concatenate(pcs, axis=0), vc_ref[...], preferred_element_type=F32)
    sel = _nsa_select(imp, posq, lane).astype(BF16)

    m_ref[...] = jnp.full_like(m_ref, NEG_INF)
    l_ref[...] = jnp.zeros_like(l_ref)
    acc_ref[...] = jnp.zeros_like(acc_ref)
    ri = lax.broadcasted_iota(jnp.int32, (tq, tk), 0)
    ci = lax.broadcasted_iota(jnp.int32, (tq, tk), 1)
    bj = lax.broadcasted_iota(jnp.int32, (128, tk), 0)
    bk = lax.broadcasted_iota(jnp.int32, (128, tk), 1)

    def make_chunk(br):
        def chunk(kc, carry):
            k0 = pl.multiple_of(kc * tk, tk)
            s_e = _dot_nt(q4, ke_ref[br, pl.ds(k0, tk), :])
            s_o = _dot_nt(q4, ko_ref[br, pl.ds(k0, tk), :])
            dist = (q0 - k0) + ri - ci
            distf = dist.astype(F32)
            if br == 0:
                expand = jnp.where(lax.shift_right_logical(k0 + bk, 6) == bj, 1.0, 0.0).astype(BF16)
                selk = jnp.dot(sel, expand, preferred_element_type=F32)
                mask = jnp.where(dist >= 0, selk, 0.0) > 0.5
            else:
                mask = jnp.where(dist >= 0, dist, NSA_WINDOW) < NSA_WINDOW
            ps, alphas = [], []
            for h in range(NSA_HEADS):
                j, e = divmod(h, 2)
                sc = (s_e if e == 0 else s_o)[j * tq:(j + 1) * tq] * (NSA_DH ** -0.5) - slopes[h] * distf
                p, alpha = _softmax_step(sc, mask, m_ref, l_ref, br * NSA_HEADS + h)
                ps.append(p.astype(BF16))
                alphas.append(alpha)
            pv = jnp.dot(jnp.concatenate(ps, axis=0), vd_ref[br, pl.ds(k0, tk), :], preferred_element_type=F32)
            for h in range(NSA_HEADS):
                i = br * NSA_HEADS + h
                acc_ref[i] = alphas[h] * acc_ref[i] + pv[h * tq:(h + 1) * tq]
            return carry
        return chunk

    n_chunks = (q0 + tq + tk - 1) // tk
    lax.fori_loop(0, n_chunks, make_chunk(0), 0)
    lax.fori_loop(jnp.maximum(q0 - (NSA_WINDOW - 1), 0) // tk, n_chunks, make_chunk(1), 0)

    gs = jax.nn.sigmoid(sm_ref[...])
    outs = []
    for j in range(4):
        pair = []
        for e in range(2):
            h = 2 * j + e
            c0 = MLA_D_ROPE + 3 * h
            o_s = acc_ref[h] / jnp.maximum(l_ref[h], TINY)
            o_w = acc_ref[NSA_HEADS + h] / jnp.maximum(l_ref[NSA_HEADS + h], TINY)
            pair.append(gs[:, c0:c0 + 1] * oc[h * tq:(h + 1) * tq] + gs[:, c0 + 1:c0 + 2] * o_s
                        + gs[:, c0 + 2:c0 + 3] * o_w)
        outs.append(jnp.where(lane < NSA_DH, pair[0], pair[1]))
    o_ref[...] = _rms(jnp.concatenate(outs, axis=-1), g_ref[...])


def _nsa_prompt(u, g, *, b, t, tq=128, tk=256):
    nq = t // tq
    return pl.pallas_call(
        functools.partial(_nsa_prompt_body, t=t, tq=tq, tk=tk),
        grid=(b, nq),
        in_specs=[
            pl.BlockSpec((tq, 512), lambda bi, qi: (bi * nq + qi, COL_NQ // 512)),
            pl.BlockSpec((tq, 128), lambda bi, qi: (bi * nq + qi, COL_SMALL // 128)),
            pl.BlockSpec((t, 384), lambda bi, qi: (bi, COL_NKV // 384)),
            pl.BlockSpec((1, GROUP_W), lambda bi, qi: (0, 0)),
        ],
        out_specs=pl.BlockSpec((tq, GROUP_W), lambda bi, qi: (bi * nq + qi, 0)),
        out_shape=jax.ShapeDtypeStruct((b * t, GROUP_W), F32),
        scratch_shapes=[pltpu.VMEM((256, 128), BF16), pltpu.VMEM((128, 128), BF16),
                        pltpu.VMEM((2, t, 128), BF16), pltpu.VMEM((2, t, 128), BF16), pltpu.VMEM((2, t, 128), BF16),
                        pltpu.VMEM((2 * NSA_HEADS, tq, 1), F32), pltpu.VMEM((2 * NSA_HEADS, tq, 1), F32),
                        pltpu.VMEM((2 * NSA_HEADS, tq, 128), F32)],
        compiler_params=pltpu.CompilerParams(
            dimension_semantics=("parallel", "arbitrary"), vmem_limit_bytes=VMEM_LIMIT),
        name="nsa_prompt",
    )(u, u, u, g.reshape(1, GROUP_W))


TS_PAD = 8
PAGES_PER_STEP = 16


def _page_specs(layer, width, pps):
    def spec(j):
        return pl.BlockSpec((None, None, 128, width), lambda b, g, pt: (layer, pt[b, g * pps + j], 0, 0))
    return [spec(j) for j in range(pps)]


def _pad_keys(x):
    return jnp.concatenate([x, jnp.zeros((128 - TS_PAD, x.shape[1]), x.dtype)], axis=0)


def _diff_decode_body(pt_ref, lv_ref, q_ref, new_ref, g_ref, *rest, pps, ts, past_len, lam_init):
    pages = rest[:pps]
    o_ref, qst_ref, m_ref, l_ref, acc_ref = rest[pps:]
    g = pl.program_id(1)
    rows = 2 * DIFF_HEADS * TS_PAD
    w = pps * 128

    @pl.when(g == 0)
    def _():
        qst_ref[...] = _diff_streams(q_ref[...], TS_PAD)
        m_ref[...] = jnp.full_like(m_ref, NEG_INF)
        l_ref[...] = jnp.zeros_like(l_ref)
        acc_ref[...] = jnp.zeros_like(acc_ref)

    qst = qst_ref[...]
    ri = lax.broadcasted_iota(jnp.int32, (rows, 1), 0)
    head = (ri >> 3) & (DIFF_HEADS - 1)
    sl = [2.0 ** (-8.0 * (h + 1) / DIFF_HEADS) for h in range(DIFF_HEADS)]
    slope = jnp.where(head == 0, sl[0], jnp.where(head == 1, sl[1], jnp.where(head == 2, sl[2], sl[3])))
    tok = ri & (TS_PAD - 1)

    def update(sc, mask, vals):
        if mask is not None:
            sc = jnp.where(mask, sc, NEG_INF)
        m_old = m_ref[...]
        m_new = jnp.maximum(m_old, jnp.max(sc, axis=-1, keepdims=True))
        p = jnp.exp(sc - m_new)
        if mask is not None:
            p = jnp.where(mask, p, 0.0)
        alpha = jnp.exp(m_old - m_new)
        l_ref[...] = alpha * l_ref[...] + jnp.sum(p, axis=-1, keepdims=True)
        m_ref[...] = m_new
        pb = p.astype(BF16)
        pv = jnp.dot(pb[:, 0:128], vals[0], preferred_element_type=F32)
        for j in range(1, len(vals)):
            pv += jnp.dot(pb[:, j * 128:(j + 1) * 128], vals[j], preferred_element_type=F32)
        acc_ref[...] = alpha * acc_ref[...] + pv

    s = jnp.concatenate([_dot_nt(qst, pages[j][:, 0:128].astype(BF16)) for j in range(pps)], axis=1)
    ci = lax.broadcasted_iota(jnp.int32, (rows, w), 1)
    distf = ((past_len - g * w) + tok - ci).astype(F32)
    update(s * (DIFF_DH ** -0.5) - slope * distf, None, [pages[j][:, 128:256].astype(BF16) for j in range(pps)])

    @pl.when(g == pl.num_programs(1) - 1)
    def _():
        new = _pad_keys(new_ref[...])
        sn = _dot_nt(qst, new[:, 0:128].astype(BF16))
        cn = lax.broadcasted_iota(jnp.int32, (rows, 128), 1)
        dist = tok - cn
        mask = (dist >= 0) & (cn < ts)
        update(sn * (DIFF_DH ** -0.5) - slope * dist.astype(F32), mask, [new[:, 128:256].astype(BF16)])
        lam = _diff_lambda(lv_ref, lam_init)
        acc = acc_ref[...] / jnp.maximum(l_ref[...], TINY)
        for h in range(DIFF_HEADS):
            o0 = acc[h * TS_PAD:(h + 1) * TS_PAD]
            o1 = acc[(DIFF_HEADS + h) * TS_PAD:(DIFF_HEADS + h + 1) * TS_PAD]
            o_ref[:, h * 128:(h + 1) * 128] = _rms(o0 - lam * o1, g_ref[...]) * (1.0 - lam_init)


def _diff_decode(page_table, u_s, cache, lv, subln_g, *, layer, ts, lam_init, pps=PAGES_PER_STEP):
    bs, n_pages = page_table.shape
    rows = 2 * DIFF_HEADS * TS_PAD
    grid_spec = pltpu.PrefetchScalarGridSpec(
        num_scalar_prefetch=1,
        grid=(bs, n_pages // pps),
        in_specs=[
            pl.BlockSpec((4, DIFF_DH), lambda b, g, pt: (0, 0)),
            pl.BlockSpec((None, TS_PAD, 512), lambda b, g, pt: (b, 0, COL_DQ // 512)),
            pl.BlockSpec((None, TS_PAD, 256), lambda b, g, pt: (b, 0, COL_DK // 256)),
            pl.BlockSpec((1, DIFF_DV), lambda b, g, pt: (0, 0)),
        ] + _page_specs(layer, 256, pps),
        out_specs=pl.BlockSpec((None, TS_PAD, GROUP_W), lambda b, g, pt: (b, 0, 0)),
        scratch_shapes=[pltpu.VMEM((rows, 128), BF16), pltpu.VMEM((rows, 1), F32), pltpu.VMEM((rows, 1), F32),
                        pltpu.VMEM((rows, 128), F32)],
    )
    return pl.pallas_call(
        functools.partial(_diff_decode_body, pps=pps, ts=ts, past_len=n_pages * 128, lam_init=lam_init),
        grid_spec=grid_spec,
        out_shape=jax.ShapeDtypeStruct((bs, TS_PAD, GROUP_W), F32),
        compiler_params=pltpu.CompilerParams(
            dimension_semantics=("parallel", "arbitrary"), vmem_limit_bytes=VMEM_LIMIT),
        name="diff_decode",
    )(page_table, lv, u_s, u_s, subln_g.reshape(1, DIFF_DV), *([cache] * pps))


def _mla_decode_body(pt_ref, q_ref, new_ref, wuv_ref, g_ref, *rest, pps, ts):
    pages = rest[:pps]
    o_ref, qst_ref, m_ref, l_ref, acc_ref = rest[pps:]
    g = pl.program_id(1)
    rows = MLA_HEADS * TS_PAD
    scale = (MLA_D_NOPE + MLA_D_ROPE) ** -0.5

    @pl.when(g == 0)
    def _():
        qst_ref[...] = jnp.concatenate([q_ref[:, h * 256:(h + 1) * 256] for h in range(MLA_HEADS)], axis=0)
        m_ref[...] = jnp.full_like(m_ref, NEG_INF)
        l_ref[...] = jnp.zeros_like(l_ref)
        acc_ref[...] = jnp.zeros_like(acc_ref)

    q_lat = qst_ref[:, 0:MLA_KV_RANK]
    q_rope = qst_ref[:, MLA_KV_RANK:MLA_KV_RANK + MLA_D_ROPE]

    def scores(lat, kr):
        return (_dot_nt(q_lat, lat) + _dot_nt(q_rope, kr)) * scale

    def update(sc, mask, vals):
        if mask is not None:
            sc = jnp.where(mask, sc, NEG_INF)
        m_old = m_ref[...]
        m_new = jnp.maximum(m_old, jnp.max(sc, axis=-1, keepdims=True))
        p = jnp.exp(sc - m_new)
        if mask is not None:
            p = jnp.where(mask, p, 0.0)
        alpha = jnp.exp(m_old - m_new)
        l_ref[...] = alpha * l_ref[...] + jnp.sum(p, axis=-1, keepdims=True)
        m_ref[...] = m_new
        pb = p.astype(BF16)
        pv = jnp.dot(pb[:, 0:128], vals[0], preferred_element_type=F32)
        for j in range(1, len(vals)):
            pv += jnp.dot(pb[:, j * 128:(j + 1) * 128], vals[j], preferred_element_type=F32)
        acc_ref[...] = alpha * acc_ref[...] + pv

    lats = [pages[j][:, 0:MLA_KV_RANK].astype(BF16) for j in range(pps)]
    s = jnp.concatenate(
        [scores(lats[j], pages[j][:, MLA_KV_RANK:MLA_KV_RANK + MLA_D_ROPE].astype(BF16)) for j in range(pps)], axis=1)
    update(s, None, lats)

    @pl.when(g == pl.num_programs(1) - 1)
    def _():
        new = _pad_keys(new_ref[...])
        lat = new[:, 0:MLA_KV_RANK].astype(BF16)
        sn = scores(lat, new[:, MLA_KV_RANK:MLA_KV_RANK + MLA_D_ROPE].astype(BF16))
        tok = lax.broadcasted_iota(jnp.int32, (rows, 128), 0) & (TS_PAD - 1)
        cn = lax.broadcasted_iota(jnp.int32, (rows, 128), 1)
        update(sn, (cn <= tok) & (cn < ts), [lat])
        acc = acc_ref[...] / jnp.maximum(l_ref[...], TINY)
        outs = [jnp.dot(acc[h * TS_PAD:(h + 1) * TS_PAD].astype(BF16), wuv_ref[h], preferred_element_type=F32)
                for h in range(MLA_HEADS)]
        o_ref[...] = _rms(jnp.concatenate(outs, axis=-1), g_ref[...])


def _mla_decode(page_table, q_s, k_s, cache, wuv, g, *, layer, ts, pps=PAGES_PER_STEP):
    bs, n_pages = page_table.shape
    rows = MLA_HEADS * TS_PAD
    width = cache.shape[-1]
    grid_spec = pltpu.PrefetchScalarGridSpec(
        num_scalar_prefetch=1,
        grid=(bs, n_pages // pps),
        in_specs=[
            pl.BlockSpec((None, TS_PAD, MLA_HEADS * 256), lambda b, g, pt: (b, 0, 0)),
            pl.BlockSpec((None, TS_PAD, 256), lambda b, g, pt: (b, 0, 0)),
            pl.BlockSpec((MLA_HEADS, 128, 128), lambda b, g, pt: (0, 0, 0)),
            pl.BlockSpec((1, GROUP_W), lambda b, g, pt: (0, 0)),
        ] + _page_specs(layer, width, pps),
        out_specs=pl.BlockSpec((None, TS_PAD, GROUP_W), lambda b, g, pt: (b, 0, 0)),
        scratch_shapes=[pltpu.VMEM((rows, 256), BF16), pltpu.VMEM((rows, 1), F32), pltpu.VMEM((rows, 1), F32),
                        pltpu.VMEM((rows, 128), F32)],
    )
    return pl.pallas_call(
        functools.partial(_mla_decode_body, pps=pps, ts=ts),
        grid_spec=grid_spec,
        out_shape=jax.ShapeDtypeStruct((bs, TS_PAD, GROUP_W), F32),
        compiler_params=pltpu.CompilerParams(
            dimension_semantics=("parallel", "arbitrary"), vmem_limit_bytes=VMEM_LIMIT),
        name="mla_decode",
    )(page_table, q_s, k_s, wuv, g.reshape(1, GROUP_W), *([cache] * pps))


def _split_kv(kv):
    lo = lax.broadcasted_iota(jnp.int32, kv.shape, 1) < NSA_DH
    sw = pltpu.roll(kv, NSA_DH, 1)
    return (jnp.where(lo, kv, 0.0).astype(BF16), jnp.where(lo, 0.0, sw).astype(BF16),
            jnp.where(lo, sw, kv).astype(BF16))


def _nsa_decode_body(pt_ref, q_ref, sm_ref, new_ref, win_ref, g_ref, *rest, pps, ts, past_len, w_buf):
    pages = rest[:pps]
    o_ref, cm_ref, ke_ref, ko_ref, vd_ref, selm_ref, m_ref, l_ref, acc_ref = rest[pps:]
    g = pl.program_id(1)
    ck = pps * 128
    bpc = ck // NSA_BLOCK
    n_chunks = past_len // ck
    nb_past = past_len // NSA_BLOCK
    slopes = _nsa_slopes()
    scale = NSA_DH ** -0.5

    cmp_rows = jnp.concatenate([pages[j][:, 0:128] for j in range(pps)], axis=0)
    cm_ref[pl.ds(pl.multiple_of(g * bpc, bpc), bpc), :] = (
        jnp.sum(cmp_rows.reshape(bpc, NSA_BLOCK, 128), axis=1) * (1.0 / NSA_BLOCK))
    for j in range(pps):
        r0 = pl.multiple_of(g * ck + j * 128, 128)
        e, o, v = _split_kv(pages[j][:, 128:256])
        ke_ref[pl.ds(r0, 128), :] = e
        ko_ref[pl.ds(r0, 128), :] = o
        vd_ref[pl.ds(r0, 128), :] = v

    @pl.when(g == pl.num_programs(1) - 1)
    def _():
        q4 = jnp.concatenate([q_ref[:, j * 128:(j + 1) * 128] for j in range(4)], axis=0).astype(BF16)
        m_ref[...] = jnp.full_like(m_ref, NEG_INF)
        l_ref[...] = jnp.zeros_like(l_ref)
        acc_ref[...] = jnp.zeros_like(acc_ref)

        def attend(br, s_e, s_o, distf, mask, vd):
            ps, alphas = [], []
            for h in range(NSA_HEADS):
                j, e = divmod(h, 2)
                sc = (s_e if e == 0 else s_o)[j * TS_PAD:(j + 1) * TS_PAD] * scale - slopes[h] * distf
                p, alpha = _softmax_step(sc, mask, m_ref, l_ref, br * NSA_HEADS + h)
                ps.append(p.astype(BF16))
                alphas.append(alpha)
            pv = jnp.dot(jnp.concatenate(ps, axis=0), vd, preferred_element_type=F32)
            for h in range(NSA_HEADS):
                i = br * NSA_HEADS + h
                acc_ref[i] = alphas[h] * acc_ref[i] + pv[h * TS_PAD:(h + 1) * TS_PAD]

        kce, kco, vcd = _split_kv(cm_ref[...])
        s_e, s_o = _dot_nt(q4, kce), _dot_nt(q4, kco)
        blk = lax.broadcasted_iota(jnp.int32, (TS_PAD, nb_past), 1)
        tok_c = lax.broadcasted_iota(jnp.int32, (TS_PAD, nb_past), 0)
        dist_c = (past_len + tok_c) - (NSA_BLOCK * (blk + 1) - 1)
        mask_c = dist_c >= 0
        dist_cf = dist_c.astype(F32)
        imp = jnp.zeros((TS_PAD, nb_past), F32)
        pcs = []
        for h in range(NSA_HEADS):
            j, e = divmod(h, 2)
            sc = (s_e if e == 0 else s_o)[j * TS_PAD:(j + 1) * TS_PAD] * scale - slopes[h] * dist_cf
            sc = jnp.where(mask_c, sc, NEG_INF)
            ex = jnp.where(mask_c, jnp.exp(sc - jnp.max(sc, axis=-1, keepdims=True)), 0.0)
            pc = ex / jnp.maximum(jnp.sum(ex, axis=-1, keepdims=True), TINY)
            imp = imp + pc
            pcs.append(pc.astype(BF16))
        oc = jnp.dot(jnp.concatenate(pcs, axis=0), vcd, preferred_element_type=F32)

        nbl = nb_past + 128
        lane_b = lax.broadcasted_iota(jnp.int32, (TS_PAD, nbl), 1)
        posq = past_len + lax.broadcasted_iota(jnp.int32, (TS_PAD, nbl), 0)
        sel = _nsa_select(jnp.concatenate([imp, jnp.zeros((TS_PAD, 128), F32)], axis=1), posq, lane_b)
        for c in range(n_chunks):
            selm_ref[c] = sel[:, c * bpc:(c + 1) * bpc]

        tok = lax.broadcasted_iota(jnp.int32, (TS_PAD, ck), 0)
        ci = lax.broadcasted_iota(jnp.int32, (TS_PAD, ck), 1)
        expand = jnp.where(
            lax.shift_right_logical(lax.broadcasted_iota(jnp.int32, (bpc, ck), 1), 6)
            == lax.broadcasted_iota(jnp.int32, (bpc, ck), 0), 1.0, 0.0).astype(BF16)

        def chunk(c, carry):
            k0 = pl.multiple_of(c * ck, ck)
            selk = jnp.dot(selm_ref[c].astype(BF16), expand, preferred_element_type=F32)
            distf = ((past_len - k0) + tok - ci).astype(F32)
            attend(0, _dot_nt(q4, ke_ref[pl.ds(k0, ck), :]), _dot_nt(q4, ko_ref[pl.ds(k0, ck), :]), distf,
                   selk > 0.5, vd_ref[pl.ds(k0, ck), :])
            return carry

        lax.fori_loop(0, n_chunks, chunk, 0)

        tok_n = lax.broadcasted_iota(jnp.int32, (TS_PAD, 128), 0)
        cn = lax.broadcasted_iota(jnp.int32, (TS_PAD, 128), 1)
        dist_n = tok_n - cn
        causal_n = (dist_n >= 0) & (cn < ts)
        dist_nf = dist_n.astype(F32)
        new = _pad_keys(new_ref[...])
        e, o, v = _split_kv(new[:, 128:256])
        attend(0, _dot_nt(q4, e), _dot_nt(q4, o), dist_nf,
               causal_n & (sel[:, nb_past:nb_past + 1] > 0.5), v)
        e, o, v = _split_kv(new[:, 256:384])
        attend(1, _dot_nt(q4, e), _dot_nt(q4, o), dist_nf, causal_n, v)

        e, o, v = _split_kv(win_ref[...])
        dist_w = (lax.broadcasted_iota(jnp.int32, (TS_PAD, w_buf), 0) + w_buf
                  - lax.broadcasted_iota(jnp.int32, (TS_PAD, w_buf), 1))
        attend(1, _dot_nt(q4, e), _dot_nt(q4, o), dist_w.astype(F32), dist_w < NSA_WINDOW, v)

        gs = jax.nn.sigmoid(sm_ref[...])
        outs = []
        for j in range(4):
            pair = []
            for e in range(2):
                h = 2 * j + e
                c0 = MLA_D_ROPE + 3 * h
                o_s = acc_ref[h] / jnp.maximum(l_ref[h], TINY)
                o_w = acc_ref[NSA_HEADS + h] / jnp.maximum(l_ref[NSA_HEADS + h], TINY)
                pair.append(gs[:, c0:c0 + 1] * oc[h * TS_PAD:(h + 1) * TS_PAD] + gs[:, c0 + 1:c0 + 2] * o_s
                            + gs[:, c0 + 2:c0 + 3] * o_w)
            outs.append(jnp.where(cn < NSA_DH, pair[0], pair[1]))
        o_ref[...] = _rms(jnp.concatenate(outs, axis=-1), g_ref[...])


def _nsa_decode(page_table, u_s, cache, cache_win, g, *, layer, ts, pps=PAGES_PER_STEP):
    bs, n_pages = page_table.shape
    past_len = n_pages * 128
    w_buf = cache_win.shape[2]
    assert ts < NSA_BLOCK and past_len % (pps * 128) == 0 and w_buf == NSA_WINDOW
    nst = 2 * NSA_HEADS
    grid_spec = pltpu.PrefetchScalarGridSpec(
        num_scalar_prefetch=1,
        grid=(bs, n_pages // pps),
        in_specs=[
            pl.BlockSpec((None, TS_PAD, 512), lambda b, g, pt: (b, 0, COL_NQ // 512)),
            pl.BlockSpec((None, TS_PAD, 128), lambda b, g, pt: (b, 0, COL_SMALL // 128)),
            pl.BlockSpec((None, TS_PAD, 384), lambda b, g, pt: (b, 0, COL_NKV // 384)),
            pl.BlockSpec((None, None, w_buf, 128), lambda b, g, pt: (layer, b, 0, 0)),
            pl.BlockSpec((1, GROUP_W), lambda b, g, pt: (0, 0)),
        ] + _page_specs(layer, 256, pps),
        out_specs=pl.BlockSpec((None, TS_PAD, GROUP_W), lambda b, g, pt: (b, 0, 0)),
        scratch_shapes=[pltpu.VMEM((past_len // NSA_BLOCK, 128), F32),
                        pltpu.VMEM((past_len, 128), BF16), pltpu.VMEM((past_len, 128), BF16),
                        pltpu.VMEM((past_len, 128), BF16),
                        pltpu.VMEM((n_pages // pps, TS_PAD, pps * 128 // NSA_BLOCK), F32),
                        pltpu.VMEM((nst, TS_PAD, 1), F32), pltpu.VMEM((nst, TS_PAD, 1), F32),
                        pltpu.VMEM((nst, TS_PAD, 128), F32)],
    )
    return pl.pallas_call(
        functools.partial(_nsa_decode_body, pps=pps, ts=ts, past_len=past_len, w_buf=w_buf),
        grid_spec=grid_spec,
        out_shape=jax.ShapeDtypeStruct((bs, TS_PAD, GROUP_W), F32),
        compiler_params=pltpu.CompilerParams(
            dimension_semantics=("parallel", "arbitrary"), vmem_limit_bytes=VMEM_LIMIT),
        name="nsa_decode",
    )(page_table, u_s, u_s, u_s, cache_win, g.reshape(1, GROUP_W), *([cache] * pps))


SSD_DT_LANE = MLA_D_ROPE + 3 * NSA_HEADS


def _ssd_body(z_ref, x_ref, sm_ref, tail0_ref, h0_ref, cw_ref, cb_ref, dtb_ref, al_ref, dsk_ref, g_ref,
              o_ref, h_ref, tail_ref, *, q, rows, n_valid):
    @pl.when(pl.program_id(1) == 0)
    def _():
        tail_ref[...] = tail0_ref[...]
        h_ref[...] = h0_ref[...]

    def padded(ref):
        v = ref[...]
        return v if rows == q else jnp.concatenate([v, jnp.zeros((q - rows, v.shape[1]), v.dtype)], axis=0)

    x = padded(x_ref)
    xcat = jnp.concatenate([tail_ref[...], x], axis=0)
    conv = cb_ref[...] + cw_ref[3:4, :] * x
    for k in range(1, SSD_CONV):
        conv = conv + cw_ref[3 - k:4 - k, :] * pltpu.roll(xcat, k, 0)[8:]
    tail_ref[...] = x[q - 8:q]
    xbc = conv * jax.nn.sigmoid(conv)
    xs = xbc[:, 0:GROUP_W]
    xs_b = xs.astype(BF16)
    b_b = xbc[:, GROUP_W:GROUP_W + 256].astype(BF16)
    c_b = xbc[:, GROUP_W + 256:GROUP_W + 512].astype(BF16)

    lane = lax.broadcasted_iota(jnp.int32, (q, 128), 1)
    row = lax.broadcasted_iota(jnp.int32, (q, 128), 0)
    xdt = padded(sm_ref) + dtb_ref[...]
    dt = jnp.maximum(xdt, 0.0) + jnp.log1p(jnp.exp(-jnp.abs(xdt)))
    if n_valid < q:
        dt = jnp.where(row < n_valid, dt, 0.0)
    is_dt = (lane >= SSD_DT_LANE) & (lane < SSD_DT_LANE + SSD_HEADS)
    acum = jnp.where(is_dt, dt * -jnp.exp(al_ref[...]), 0.0)
    s = 1
    while s < q:
        acum = acum + jnp.where(row >= s, pltpu.roll(acum, s, 0), 0.0)
        s *= 2
    acum_t = acum.T
    dt_t = dt.T
    xs_t = xs.T
    causal = lax.broadcasted_iota(jnp.int32, (q, q), 0) >= lax.broadcasted_iota(jnp.int32, (q, q), 1)
    top = lax.broadcasted_iota(jnp.int32, (128, 128), 0) < SSD_HEADDIM
    top_q = lax.broadcasted_iota(jnp.int32, (128, q), 0) < SSD_HEADDIM
    ys = []
    for j in range(SSD_HEADS // 2):
        gi = (2 * j) // (SSD_HEADS // SSD_GROUPS)
        cg = c_b[:, gi * 128:(gi + 1) * 128]
        bg = b_b[:, gi * 128:(gi + 1) * 128]
        cb = _dot_nt(cg, bg)
        xpair = xs_b[:, j * 128:(j + 1) * 128]
        hp = h_ref[j]
        inter = _dot_nt(cg, hp.astype(BF16))
        ypair, to_end, last = [], [], []
        for e in range(2):
            ln = SSD_DT_LANE + 2 * j + e
            col = acum[:, ln:ln + 1]
            rowv = acum_t[ln:ln + 1, :]
            dtr = dt_t[ln:ln + 1, :]
            decay = jnp.where(causal, jnp.exp(jnp.where(causal, col - rowv, 0.0)), 0.0)
            w = (cb * decay * dtr).astype(BF16)
            ypair.append(jnp.dot(w, xpair, preferred_element_type=F32) + inter * jnp.exp(col))
            a_last = rowv[:, q - 1:q]
            to_end.append(jnp.exp(a_last - rowv) * dtr)
            last.append(jnp.exp(a_last))
        ys.append(jnp.where(lane < SSD_HEADDIM, ypair[0], ypair[1]))
        xw = (xs_t[j * 128:(j + 1) * 128, :] * jnp.where(top_q, to_end[0], to_end[1])).astype(BF16)
        h_ref[j] = hp * jnp.where(top, last[0], last[1]) + jnp.dot(xw, bg, preferred_element_type=F32)
    y = (jnp.concatenate(ys, axis=-1) + dsk_ref[...] * xs)[0:rows]
    zz = z_ref[...]
    o_ref[...] = _rms(y * (zz * jax.nn.sigmoid(zz)), g_ref[...])


def _ssd_params(conv_b, dt_bias, a_log, d_skip, norm_g):
    lanes = lambda v: jnp.zeros((1, 128), F32).at[0, SSD_DT_LANE:SSD_DT_LANE + SSD_HEADS].set(v)
    return (conv_b.reshape(1, -1), lanes(dt_bias), lanes(a_log),
            jnp.repeat(d_skip, SSD_HEADDIM).reshape(1, GROUP_W), norm_g.reshape(1, GROUP_W))


def _ssd_param_specs(fixed):
    return [pl.BlockSpec((SSD_CONV, SSD_CONV_CH), fixed), pl.BlockSpec((1, SSD_CONV_CH), fixed),
            pl.BlockSpec((1, 128), fixed), pl.BlockSpec((1, 128), fixed),
            pl.BlockSpec((1, GROUP_W), fixed), pl.BlockSpec((1, GROUP_W), fixed)]


def _ssd_decode(u_s, conv_state, h0, conv_w, conv_b, dt_bias, a_log, d_skip, norm_g, *, layer, ts, q=SSD_CHUNK):
    bs = u_s.shape[0]
    fixed = lambda bi, ci: (0, 0)
    o, h = pl.pallas_call(
        functools.partial(_ssd_body, q=q, rows=TS_PAD, n_valid=ts),
        grid=(bs, 1),
        in_specs=[
            pl.BlockSpec((None, TS_PAD, GROUP_W), lambda bi, ci: (bi, 0, COL_SZ // GROUP_W)),
            pl.BlockSpec((None, TS_PAD, SSD_CONV_CH), lambda bi, ci: (bi, 0, COL_SXBC // SSD_CONV_CH)),
            pl.BlockSpec((None, TS_PAD, 128), lambda bi, ci: (bi, 0, COL_SMALL // 128)),
            pl.BlockSpec((None, 8, SSD_CONV_CH), lambda bi, ci: (bi, 0, 0)),
            pl.BlockSpec((None, None, SSD_HEADS // 2, 128, 128), lambda bi, ci: (layer, bi, 0, 0, 0)),
        ] + _ssd_param_specs(fixed),
        out_specs=[pl.BlockSpec((None, TS_PAD, GROUP_W), lambda bi, ci: (bi, 0, 0)),
                   pl.BlockSpec((None, SSD_HEADS // 2, 128, 128), lambda bi, ci: (bi, 0, 0, 0))],
        out_shape=[jax.ShapeDtypeStruct((bs, TS_PAD, GROUP_W), F32),
                   jax.ShapeDtypeStruct((bs, SSD_HEADS // 2, 128, 128), F32)],
        scratch_shapes=[pltpu.VMEM((8, SSD_CONV_CH), F32)],
        compiler_params=pltpu.CompilerParams(
            dimension_semantics=("parallel", "arbitrary"), vmem_limit_bytes=VMEM_LIMIT),
        name="ssd_decode",
    )(u_s, u_s, u_s, conv_state, h0, conv_w, *_ssd_params(conv_b, dt_bias, a_log, d_skip, norm_g))
    return o, h.reshape(bs, SSD_HEADS, SSD_HEADDIM, SSD_STATE)


def _ssd_prompt(u, conv_w, conv_b, dt_bias, a_log, d_skip, norm_g, *, b, t, q=SSD_CHUNK):
    nc = t // q
    fixed = lambda bi, ci: (0, 0)
    o, h = pl.pallas_call(
        functools.partial(_ssd_body, q=q, rows=q, n_valid=q),
        grid=(b, nc),
        in_specs=[
            pl.BlockSpec((q, GROUP_W), lambda bi, ci: (bi * nc + ci, COL_SZ // GROUP_W)),
            pl.BlockSpec((q, SSD_CONV_CH), lambda bi, ci: (bi * nc + ci, COL_SXBC // SSD_CONV_CH)),
            pl.BlockSpec((q, 128), lambda bi, ci: (bi * nc + ci, COL_SMALL // 128)),
            pl.BlockSpec((8, SSD_CONV_CH), fixed),
            pl.BlockSpec((SSD_HEADS // 2, 128, 128), lambda bi, ci: (0, 0, 0)),
            pl.BlockSpec((SSD_CONV, SSD_CONV_CH), fixed),
            pl.BlockSpec((1, SSD_CONV_CH), fixed),
            pl.BlockSpec((1, 128), fixed),
            pl.BlockSpec((1, 128), fixed),
            pl.BlockSpec((1, GROUP_W), fixed),
            pl.BlockSpec((1, GROUP_W), fixed),
        ],
        out_specs=[pl.BlockSpec((q, GROUP_W), lambda bi, ci: (bi * nc + ci, 0)),
                   pl.BlockSpec((None, SSD_HEADS // 2, 128, 128), lambda bi, ci: (bi, 0, 0, 0))],
        out_shape=[jax.ShapeDtypeStruct((b * t, GROUP_W), F32),
                   jax.ShapeDtypeStruct((b, SSD_HEADS // 2, 128, 128), F32)],
        scratch_shapes=[pltpu.VMEM((8, SSD_CONV_CH), F32)],
        compiler_params=pltpu.CompilerParams(
            dimension_semantics=("parallel", "arbitrary"), vmem_limit_bytes=VMEM_LIMIT),
        name="ssd_prompt",
    )(u, u, u, jnp.zeros((8, SSD_CONV_CH), F32), jnp.zeros((SSD_HEADS // 2, 128, 128), F32), conv_w,
      *_ssd_params(conv_b, dt_bias, a_log, d_skip, norm_g))
    return o, h.reshape(b, SSD_HEADS, SSD_HEADDIM, SSD_STATE)


def _rms_j(x, g):
    x32 = x.astype(F32)
    return x32 * lax.rsqrt(jnp.mean(x32 * x32, axis=-1, keepdims=True) + EPS) * g.astype(F32)


def _masked_softmax(s, mask):
    s = jnp.where(mask, s.astype(F32), NEG_INF)
    e = jnp.exp(s - jnp.max(s, axis=-1, keepdims=True)) * mask
    return e / jnp.maximum(jnp.sum(e, axis=-1, keepdims=True), TINY)


def _alibi(n):
    return 2.0 ** (-8.0 * jnp.arange(1, n + 1, dtype=F32) / n)


def _rope(x, pos):
    half = x.shape[-1] // 2
    inv = ROPE_THETA ** (-jnp.arange(half, dtype=F32) / half)
    ang = pos.astype(F32)[:, None] * inv
    ang = ang.reshape(ang.shape[0], *([1] * (x.ndim - 3)), half)
    cos, sin = jnp.cos(ang), jnp.sin(ang)
    x1, x2 = x[..., :half], x[..., half:]
    return jnp.concatenate([x1 * cos - x2 * sin, x1 * sin + x2 * cos], axis=-1)


def _over_q(fn, q_pos, *q_args):
    T = q_pos.shape[0]
    blk = Q_BLOCK if T % Q_BLOCK == 0 else T
    nb = T // blk

    def split(a):
        return jnp.moveaxis(a.reshape(a.shape[0], nb, blk, *a.shape[2:]), 1, 0)

    out = lax.map(lambda args: fn(*args), (q_pos.reshape(nb, blk),) + tuple(split(a) for a in q_args))
    out = jnp.moveaxis(out, 0, 1)
    return out.reshape(out.shape[0], T, *out.shape[3:])


def _diff_attention(q, k, v, q_pos, k_pos, lam, lam_init, subln_g):
    slopes = _alibi(DIFF_HEADS)[:, None, None]

    def block(pb, qb):
        s = jnp.einsum('bqhcd,bkcd->bchqk', qb, k) * DIFF_DH ** -0.5
        dist = pb[:, None] - k_pos[None, :]
        s = s - slopes * dist.astype(F32)
        p = _masked_softmax(s, dist >= 0)
        w = p[:, 0] - lam * p[:, 1]
        return jnp.einsum('bhqk,bkd->bqhd', w, v)

    o = _over_q(block, q_pos, q)
    o = _rms_j(o, subln_g) * (1.0 - lam_init)
    return o.reshape(o.shape[0], o.shape[1], -1)


def _mla_attention(q_lat, q_rope, ckv, krope, q_pos, k_pos):
    scale = (MLA_D_NOPE + MLA_D_ROPE) ** -0.5

    def block(pb, qlb, qrb):
        s = (jnp.einsum('bqhr,bkr->bhqk', qlb, ckv) + jnp.einsum('bqhd,bkd->bhqk', qrb, krope)) * scale
        p = _masked_softmax(s, k_pos[None, :] <= pb[:, None])
        return jnp.einsum('bhqk,bkr->bqhr', p, ckv)

    return _over_q(block, q_pos, q_lat, q_rope)


def _nsa_attention(q, gates, kc_seq, vc_seq, ks_seq, vs_seq, kw_seq, vw_seq, q_pos, w_start):
    B = q.shape[0]
    Tk = kc_seq.shape[1]
    nblk = -(-Tk // NSA_BLOCK)
    pad = nblk * NSA_BLOCK - Tk

    def blocks(a):
        return jnp.pad(a, ((0, 0), (0, pad), (0, 0))).reshape(B, nblk, NSA_BLOCK, NSA_DH)

    kc = blocks(kc_seq).mean(axis=2)
    vc = blocks(vc_seq).mean(axis=2)
    ks, vs = blocks(ks_seq), blocks(vs_seq)
    kw = jnp.pad(kw_seq, ((0, 0), (NSA_WINDOW, 0), (0, 0)))
    vw = jnp.pad(vw_seq, ((0, 0), (NSA_WINDOW, 0), (0, 0)))
    blk_ids = jnp.arange(nblk)
    blk_end = (blk_ids + 1) * NSA_BLOCK - 1
    offs = jnp.arange(NSA_BLOCK)
    n_sel = min(NSA_TOPK, nblk)
    slopes = _alibi(NSA_HEADS)[:, None, None]
    scale = NSA_DH ** -0.5

    def block(pb, qb, gb):
        nq = pb.shape[0]
        dist_c = pb[:, None] - blk_end[None, :]
        s_c = jnp.einsum('bqhd,bnd->bhqn', qb, kc) * scale - slopes * dist_c
        p_c = _masked_softmax(s_c, dist_c >= 0)
        o_c = jnp.einsum('bhqn,bnd->bqhd', p_c, vc)
        cur = pb // NSA_BLOCK
        imp = p_c.sum(axis=1)
        forced = (blk_ids[None] == 0) | (blk_ids[None] == cur[:, None]) | (blk_ids[None] == cur[:, None] - 1)
        valid = blk_ids[None] <= cur[:, None]
        score = jnp.where(valid, jnp.where(forced, FORCE_SCORE, imp), -1.0)
        top_val, sel = lax.top_k(score, n_sel)
        ks_sel = jax.vmap(lambda a, i: a[i])(ks, sel).reshape(B, nq, n_sel * NSA_BLOCK, NSA_DH)
        vs_sel = jax.vmap(lambda a, i: a[i])(vs, sel).reshape(B, nq, n_sel * NSA_BLOCK, NSA_DH)
        sel_pos = sel[..., None] * NSA_BLOCK + offs
        dist_s = (pb[None, :, None, None] - sel_pos).reshape(B, nq, n_sel * NSA_BLOCK)
        mask_s = ((dist_s >= 0) & jnp.repeat(top_val >= 0, NSA_BLOCK, axis=-1))[:, None]
        s_s = jnp.einsum('bqhd,bqkd->bhqk', qb, ks_sel) * scale - slopes * dist_s[:, None]
        p_s = _masked_softmax(s_s, mask_s)
        o_s = jnp.einsum('bhqk,bqkd->bqhd', p_s, vs_sel)
        r0 = pb[0] - w_start
        kwb = lax.dynamic_slice_in_dim(kw, r0, NSA_WINDOW + nq, axis=1)
        vwb = lax.dynamic_slice_in_dim(vw, r0, NSA_WINDOW + nq, axis=1)
        w_pos = pb[0] - NSA_WINDOW + jnp.arange(NSA_WINDOW + nq)
        dist_w = pb[:, None] - w_pos[None, :]
        mask_w = (dist_w >= 0) & (dist_w < NSA_WINDOW) & (w_pos[None, :] >= w_start)
        s_w = jnp.einsum('bqhd,bkd->bhqk', qb, kwb) * scale - slopes * dist_w
        p_w = _masked_softmax(s_w, mask_w)
        o_w = jnp.einsum('bhqk,bkd->bqhd', p_w, vwb)
        g = jax.nn.sigmoid(gb.astype(F32))
        return g[..., 0:1] * o_c + g[..., 1:2] * o_s + g[..., 2:3] * o_w

    return _over_q(block, q_pos, q, gates)


def _ssd_scan(x, dt, A, Bm, Cm, h0):
    B, T, H, P = x.shape
    Q = SSD_CHUNK if T % SSD_CHUNK == 0 else T
    nc = T // Q

    def chunks(a):
        return jnp.moveaxis(a.reshape(B, nc, Q, *a.shape[2:]), 1, 0)

    causal = jnp.tril(jnp.ones((Q, Q), bool))[None, :, :, None]

    def step(h, inp):
        xc, dtc, Bc, Cc = inp
        acum = jnp.cumsum(dtc * A, axis=1)
        seg = acum[:, :, None, :] - acum[:, None, :, :]
        decay = jnp.where(causal, jnp.exp(jnp.where(causal, seg, 0.0)), 0.0)
        w = jnp.einsum('bihn,bjhn->bijh', Cc, Bc) * decay * dtc[:, None, :, :]
        y = (jnp.einsum('bijh,bjhp->bihp', w, xc)
             + jnp.einsum('bihn,bhpn->bihp', Cc, h) * jnp.exp(acum)[..., None])
        to_end = jnp.exp(acum[:, -1:, :] - acum) * dtc
        h = h * jnp.exp(acum[:, -1])[:, :, None, None] + jnp.einsum('bjh,bjhn,bjhp->bhpn', to_end, Bc, xc)
        return h, y

    h, ys = lax.scan(step, h0, (chunks(x), chunks(dt), chunks(Bm), chunks(Cm)))
    return jnp.moveaxis(ys, 0, 1).reshape(B, T, H, P), h


def _ssd_mixer(z, xbc, dt_raw, conv_state, h0, conv_w, conv_b, dt_bias, a_log, d_skip, norm_g):
    B, T, _ = xbc.shape
    xpad = jnp.concatenate([conv_state, xbc], axis=1)
    new_conv = xpad[:, -(SSD_CONV - 1):]
    conv = lax.conv_general_dilated(xpad, conv_w[:, None, :], window_strides=(1,),
                                    padding='VALID', dimension_numbers=('NWC', 'WIO', 'NWC'),
                                    feature_group_count=SSD_CONV_CH) + conv_b
    xbc = jax.nn.silu(conv)
    xs, Bm, Cm = jnp.split(xbc, [GROUP_W, GROUP_W + SSD_GROUPS * SSD_STATE], axis=-1)
    xs = xs.reshape(B, T, SSD_HEADS, SSD_HEADDIM)
    rep = SSD_HEADS // SSD_GROUPS
    Bm = jnp.repeat(Bm.reshape(B, T, SSD_GROUPS, SSD_STATE), rep, axis=2)
    Cm = jnp.repeat(Cm.reshape(B, T, SSD_GROUPS, SSD_STATE), rep, axis=2)
    dt = jax.nn.softplus(dt_raw + dt_bias)
    A = -jnp.exp(a_log)
    y, h = _ssd_scan(xs, dt, A, Bm, Cm, h0)
    y = y + d_skip[:, None] * xs
    y = y.reshape(B, T, GROUP_W) * jax.nn.silu(z)
    return _rms_j(y, norm_g), new_conv, h


def _mixers(u, l, p, past, past_len, w_buf):
    B, T, _ = u.shape
    pos = past_len + jnp.arange(T)
    dq = u[..., COL_DQ:COL_DQ + 512]
    nq = u[..., COL_NQ:COL_NQ + 512]
    sz = u[..., COL_SZ:COL_SZ + 512]
    new_diff = u[..., COL_DK:COL_DK + 256]
    mckv = u[..., COL_MCKV:COL_MCKV + 128]
    mkr = u[..., COL_SMALL:COL_SMALL + 32]
    ngate = u[..., COL_SMALL + 32:COL_SMALL + 56]
    sdt = u[..., COL_SMALL + 56:COL_SMALL + 64]
    sxbc = u[..., COL_SXBC:COL_SXBC + 1024]
    mcq = u[..., COL_MCQ:COL_MCQ + 384]
    nkv = u[..., COL_NKV:COL_NKV + 384]

    new_mla = jnp.concatenate([_rms_j(mckv, p['mla_kv_norm_g']), _rope(mkr, pos)], axis=-1)
    new_nsa = nkv[..., :4 * NSA_DH]
    new_win_rows = nkv[..., 4 * NSA_DH:]

    if past is None:
        full_diff, full_mla, full_nsa, win_seq = new_diff, new_mla, new_nsa, new_win_rows
        w_start = 0
        conv_state = jnp.zeros((B, SSD_CONV - 1, SSD_CONV_CH), F32)
        h0 = jnp.zeros((B, SSD_HEADS, SSD_HEADDIM, SSD_STATE), F32)
    else:
        full_diff = jnp.concatenate([past['diff'], new_diff], axis=1)
        full_mla = jnp.concatenate([past['mla'], new_mla], axis=1)
        full_nsa = jnp.concatenate([past['nsa'], new_nsa], axis=1)
        win_seq = jnp.concatenate([past['win'], new_win_rows], axis=1)
        w_start = past_len - past['win'].shape[1]
        conv_state, h0 = past['conv'], past['h']
    Tk = full_diff.shape[1]
    k_pos = jnp.arange(Tk)
    new_win = jnp.pad(win_seq, ((0, 0), (max(0, w_buf - win_seq.shape[1]), 0), (0, 0)))[:, -w_buf:]

    lv = p['diff_lambda']
    lam_init = 0.8 - 0.6 * math.exp(-0.3 * l)
    lam = jnp.exp(jnp.sum(lv[0] * lv[1])) - jnp.exp(jnp.sum(lv[2] * lv[3])) + lam_init
    o_a = _diff_attention(dq.reshape(B, T, DIFF_HEADS, 2, DIFF_DH),
                          full_diff[..., :2 * DIFF_DH].reshape(B, Tk, 2, DIFF_DH),
                          full_diff[..., 2 * DIFF_DH:], pos, k_pos, lam, lam_init, p['diff_subln_g'])

    qm = (_rms_j(mcq, p['mla_q_norm_g']) @ p['mla_w_uq']).reshape(B, T, MLA_HEADS, MLA_D_NOPE + MLA_D_ROPE)
    q_nope, q_rope = qm[..., :MLA_D_NOPE], _rope(qm[..., MLA_D_NOPE:], pos)
    q_lat = jnp.einsum('bqhn,rhn->bqhr', q_nope, p['mla_w_uk'])
    o_lat = _mla_attention(q_lat, q_rope, full_mla[..., :MLA_KV_RANK], full_mla[..., MLA_KV_RANK:], pos, k_pos)
    o_b = jnp.einsum('bqhr,rhd->bqhd', o_lat, p['mla_w_uv']).reshape(B, T, GROUP_W)
    o_b = _rms_j(o_b, p['mla_out_g'])

    o_c = _nsa_attention(nq.reshape(B, T, NSA_HEADS, NSA_DH), ngate.reshape(B, T, NSA_HEADS, 3),
                         full_nsa[..., :NSA_DH], full_nsa[..., NSA_DH:2 * NSA_DH],
                         full_nsa[..., 2 * NSA_DH:3 * NSA_DH], full_nsa[..., 3 * NSA_DH:],
                         win_seq[..., :NSA_DH], win_seq[..., NSA_DH:], pos, w_start)
    o_c = _rms_j(o_c.reshape(B, T, GROUP_W), p['nsa_out_g'])

    o_d, new_conv, new_h = _ssd_mixer(sz, sxbc, sdt, conv_state, h0, p['ssd_conv_w'], p['ssd_conv_b'],
                                      p['ssd_dt_bias'], p['ssd_a_log'], p['ssd_d'], p['ssd_norm_g'])
    return (o_a, o_b, o_c, o_d), (new_diff, new_mla, new_nsa, new_win, new_conv, new_h)


def _permute_w_in(w_in):
    def sl(a, b):
        return w_in[..., a:b]
    z64 = jnp.zeros(w_in.shape[:-1] + (64,), w_in.dtype)
    return jnp.concatenate([
        sl(0, 512), sl(1312, 1824), sl(2232, 2744), sl(512, 640), sl(640, 768), sl(1152, 1280),
        sl(1280, 1312), sl(2208, 2232), sl(3768, 3776), z64,
        sl(2744, 3768), sl(768, 1152), sl(1824, 2208)], axis=-1)


def kernel(x_prompt, x_sample, cache_diff_kv, cache_mla, cache_nsa_kv, cache_nsa_win, state_ssd_conv, state_ssd_h, page_table, ffn1_pre_g, ffn1_post_g, ffn1_w1, ffn1_w2, mix_pre_g, mix_post_g, w_in, w_out, diff_lambda, diff_subln_g, mla_q_norm_g, mla_w_uq, mla_kv_norm_g, mla_w_uk, mla_w_uv, mla_out_g, nsa_out_g, ssd_conv_w, ssd_conv_b, ssd_dt_bias, ssd_a_log, ssd_d, ssd_norm_g, ffn2_pre_g, ffn2_post_g, ffn2_w1, ffn2_w2):
    depth = w_in.shape[0]
    bp, tp, d = x_prompt.shape
    bs, ts, _ = x_sample.shape
    mp, ms = bp * tp, bs * ts
    past_len = page_table.shape[1] * cache_diff_kv.shape[2]
    w_buf = cache_nsa_win.shape[2]

    w_in_p = _permute_w_in(w_in).astype(BF16)
    w_out_b = w_out.astype(BF16).reshape(depth, 4, GROUP_W, d)
    f1w1, f1w2 = ffn1_w1.astype(BF16), ffn1_w2.astype(BF16)
    f2w1, f2w2 = ffn2_w1.astype(BF16), ffn2_w2.astype(BF16)

    mix_names = dict(diff_lambda=diff_lambda, diff_subln_g=diff_subln_g, mla_q_norm_g=mla_q_norm_g,
                     mla_w_uq=mla_w_uq, mla_kv_norm_g=mla_kv_norm_g, mla_w_uk=mla_w_uk, mla_w_uv=mla_w_uv,
                     mla_out_g=mla_out_g, nsa_out_g=nsa_out_g, ssd_conv_w=ssd_conv_w, ssd_conv_b=ssd_conv_b,
                     ssd_dt_bias=ssd_dt_bias, ssd_a_log=ssd_a_log, ssd_d=ssd_d, ssd_norm_g=ssd_norm_g)

    x = jnp.concatenate([x_prompt.reshape(mp, d), x_sample.reshape(ms, d)], axis=0)
    pos_all = jnp.concatenate([jnp.tile(jnp.arange(tp), bp), jnp.tile(past_len + jnp.arange(ts), bs)])
    cos_t, sin_t = _rope_tables(pos_all)
    st_p, st_s = [], []
    for l in range(depth):
        p = {k: v[l] for k, v in mix_names.items()}
        lam_init = 0.8 - 0.6 * math.exp(-0.3 * l)
        wuq, wuk, wuv = _mla_weight_layout(mla_w_uq[l], mla_w_uk[l], mla_w_uv[l])
        x = _ffn(x, ffn1_pre_g[l], ffn1_post_g[l], f1w1[l], f1w2[l])
        u = _in_proj(x, mix_pre_g[l], w_in_p[l])
        q_mla, k_mla = _mla_prep(u, cos_t, sin_t, mla_q_norm_g[l], mla_kv_norm_g[l], wuq, wuk)
        oa_p = _diff_prompt(u, diff_lambda[l], diff_subln_g[l], b=bp, t=tp, lam_init=lam_init)
        ob_p = _mla_prompt(q_mla, k_mla, wuv, mla_out_g[l], b=bp, t=tp)
        oc_p = _nsa_prompt(u, nsa_out_g[l], b=bp, t=tp)
        u_p = u[:mp].reshape(bp, tp, -1)
        od_p, h_p = _ssd_prompt(u, p['ssd_conv_w'], p['ssd_conv_b'], p['ssd_dt_bias'], p['ssd_a_log'], p['ssd_d'],
                                p['ssd_norm_g'], b=bp, t=tp)
        conv_p = u_p[:, tp - (SSD_CONV - 1):, COL_SXBC:COL_SXBC + SSD_CONV_CH]
        new_p = (u_p[..., COL_DK:COL_DK + 256], k_mla[:mp, :160].reshape(bp, tp, 160),
                 u_p[..., COL_NKV:COL_NKV + 256], u_p[:, tp - w_buf:, COL_NKV + 256:COL_NKV + 384], conv_p, h_p)
        o_p = (oa_p, ob_p, oc_p, od_p)
        u_s = u[mp:].reshape(bs, ts, -1)
        pad_ts = lambda a: jnp.pad(a, ((0, 0), (0, TS_PAD - ts), (0, 0)))
        u_s8 = pad_ts(u_s)
        oa_s = _diff_decode(page_table, u_s8, cache_diff_kv, diff_lambda[l], diff_subln_g[l],
                            layer=l, ts=ts, lam_init=lam_init)[:, :ts]
        ob_s = _mla_decode(page_table, pad_ts(q_mla[mp:].reshape(bs, ts, -1)), pad_ts(k_mla[mp:].reshape(bs, ts, -1)),
                           cache_mla, wuv, mla_out_g[l], layer=l, ts=ts)[:, :ts]
        nkv_s = u_s[..., COL_NKV:COL_NKV + 384]
        win_seq = jnp.concatenate([cache_nsa_win[l], nkv_s[..., 256:]], axis=1)
        oc_s = _nsa_decode(page_table, u_s8, cache_nsa_kv, cache_nsa_win, nsa_out_g[l], layer=l, ts=ts)[:, :ts]
        od_s, h_s = _ssd_decode(u_s8, jnp.pad(state_ssd_conv[l], ((0, 0), (8 - (SSD_CONV - 1), 0), (0, 0))),
                                state_ssd_h.reshape(depth, bs, SSD_HEADS // 2, 128, 128), p['ssd_conv_w'],
                                p['ssd_conv_b'], p['ssd_dt_bias'], p['ssd_a_log'], p['ssd_d'], p['ssd_norm_g'],
                                layer=l, ts=ts)
        od_s = od_s[:, :ts]
        conv_s = jnp.concatenate([state_ssd_conv[l], u_s[..., COL_SXBC:COL_SXBC + SSD_CONV_CH]],
                                 axis=1)[:, -(SSD_CONV - 1):]
        new_s = (u_s[..., COL_DK:COL_DK + 256], k_mla[mp:, :160].reshape(bs, ts, 160), nkv_s[..., :256],
                 win_seq[:, -w_buf:], conv_s, h_s)
        o_s = (oa_s, ob_s, oc_s, od_s)
        o = [jnp.concatenate([a.reshape(mp, GROUP_W), b.reshape(ms, GROUP_W)], axis=0)
             for a, b in zip(o_p, o_s)]
        x = _out_proj(x, o[0], o[1], o[2], o[3], w_out_b[l], mix_post_g[l])
        x = _ffn(x, ffn2_pre_g[l], ffn2_post_g[l], f2w1[l], f2w2[l])
        st_p.append(new_p)
        st_s.append(new_s)

    def stacked(states, i):
        return jnp.stack([s[i] for s in states])

    return (x[:mp].reshape(bp, tp, d), x[mp:].reshape(bs, ts, d),
            stacked(st_p, 0), stacked(st_s, 0),
            stacked(st_p, 1), stacked(st_s, 1),
            stacked(st_p, 2), stacked(st_s, 2),
            stacked(st_p, 3), stacked(st_s, 3),
            stacked(st_p, 4), stacked(st_s, 4),
            stacked(st_p, 5), stacked(st_s, 5))
```

```python
import functools
import math

import numpy as np
import jax
import jax.numpy as jnp
from jax import lax
from jax.experimental import pallas as pl
from jax.experimental.pallas import tpu as pltpu

D_MODEL = 2048
GROUP_W = 512
DIFF_HEADS = 4
DIFF_DH = 64
DIFF_DV = 128
MLA_HEADS = 4
MLA_DV = 128
MLA_D_NOPE = 128
MLA_D_ROPE = 32
MLA_Q_RANK = 384
MLA_KV_RANK = 128
ROPE_THETA = 10000.0
NSA_HEADS = 8
NSA_DH = 64
NSA_BLOCK = 64
NSA_TOPK = 16
NSA_WINDOW = 512
FORCE_SCORE = 1.0e4
SSD_HEADDIM = 64
SSD_HEADS = 8
SSD_STATE = 128
SSD_GROUPS = 2
SSD_CONV = 4
SSD_CHUNK = 128
SSD_CONV_CH = 1024
D_FF = 5632
Q_BLOCK = 128
EPS = 1e-6
NEG_INF = -1e30
TINY = 1e-30
F32 = jnp.float32
BF16 = jnp.bfloat16

COL_DQ = 0
COL_NQ = 512
COL_SZ = 1024
COL_DK = 1536
COL_DV = 1664
COL_MCKV = 1792
COL_SMALL = 1920
COL_SXBC = 2048
COL_MCQ = 3072
COL_NKV = 3456
N_IN_PAD = 3840

VMEM_LIMIT = 56 * 1024 * 1024


def _rms(x, g):
    return x * lax.rsqrt(jnp.mean(x * x, axis=-1, keepdims=True) + EPS) * g


def _ffn_body(x_ref, pre_ref, post_ref, wg_ref, wu_ref, w2_ref, o_ref, xn_ref, acc_ref):
    j = pl.program_id(1)

    @pl.when(j == 0)
    def _():
        xn_ref[...] = _rms(x_ref[...], pre_ref[...]).astype(BF16)
        acc_ref[...] = jnp.zeros_like(acc_ref)

    xn = xn_ref[...]
    g = jnp.dot(xn, wg_ref[...], preferred_element_type=F32)
    u = jnp.dot(xn, wu_ref[...], preferred_element_type=F32)
    h = (g * jax.nn.sigmoid(g) * u).astype(BF16)
    acc_ref[...] += jnp.dot(h, w2_ref[...], preferred_element_type=F32)

    @pl.when(j == pl.num_programs(1) - 1)
    def _():
        o_ref[...] = x_ref[...] + 0.5 * _rms(acc_ref[...], post_ref[...])


def _ffn(x, pre_g, post_g, w1, w2, *, tm=544, tf=512):
    m, d = x.shape
    nf = D_FF // tf
    return pl.pallas_call(
        _ffn_body,
        grid=(m // tm, nf),
        in_specs=[
            pl.BlockSpec((tm, d), lambda i, j: (i, 0)),
            pl.BlockSpec((1, d), lambda i, j: (0, 0)),
            pl.BlockSpec((1, d), lambda i, j: (0, 0)),
            pl.BlockSpec((d, tf), lambda i, j: (0, j)),
            pl.BlockSpec((d, tf), lambda i, j: (0, j + nf)),
            pl.BlockSpec((tf, d), lambda i, j: (j, 0)),
        ],
        out_specs=pl.BlockSpec((tm, d), lambda i, j: (i, 0)),
        out_shape=jax.ShapeDtypeStruct((m, d), F32),
        scratch_shapes=[pltpu.VMEM((tm, d), BF16), pltpu.VMEM((tm, d), F32)],
        compiler_params=pltpu.CompilerParams(
            dimension_semantics=("parallel", "arbitrary"), vmem_limit_bytes=VMEM_LIMIT),
        name="ffn",
    )(x, pre_g.reshape(1, d), post_g.reshape(1, d), w1, w1, w2)


def _in_proj_body(x_ref, g_ref, w_ref, o_ref, xn_ref):
    @pl.when(pl.program_id(1) == 0)
    def _():
        xn_ref[...] = _rms(x_ref[...], g_ref[...]).astype(BF16)

    o_ref[...] = jnp.dot(xn_ref[...], w_ref[...], preferred_element_type=F32)


def _in_proj(x, g, w, *, tm=544, tn=1280):
    m, d = x.shape
    n = w.shape[1]
    return pl.pallas_call(
        _in_proj_body,
        grid=(m // tm, n // tn),
        in_specs=[
            pl.BlockSpec((tm, d), lambda i, j: (i, 0)),
            pl.BlockSpec((1, d), lambda i, j: (0, 0)),
            pl.BlockSpec((d, tn), lambda i, j: (0, j)),
        ],
        out_specs=pl.BlockSpec((tm, tn), lambda i, j: (i, j)),
        out_shape=jax.ShapeDtypeStruct((m, n), F32),
        scratch_shapes=[pltpu.VMEM((tm, d), BF16)],
        compiler_params=pltpu.CompilerParams(
            dimension_semantics=("parallel", "arbitrary"), vmem_limit_bytes=VMEM_LIMIT),
        name="in_proj",
    )(x, g.reshape(1, d), w)


def _out_proj_body(x_ref, a_ref, b_ref, c_ref, d_ref, w_ref, g_ref, o_ref):
    mix = jnp.dot(a_ref[...].astype(BF16), w_ref[0], preferred_element_type=F32)
    mix += jnp.dot(b_ref[...].astype(BF16), w_ref[1], preferred_element_type=F32)
    mix += jnp.dot(c_ref[...].astype(BF16), w_ref[2], preferred_element_type=F32)
    mix += jnp.dot(d_ref[...].astype(BF16), w_ref[3], preferred_element_type=F32)
    o_ref[...] = x_ref[...] + _rms(mix, g_ref[...])


def _out_proj(x, o_a, o_b, o_c, o_d, w, g, *, tm=272):
    m, d = x.shape
    tok = pl.BlockSpec((tm, GROUP_W), lambda i: (i, 0))
    return pl.pallas_call(
        _out_proj_body,
        grid=(m // tm,),
        in_specs=[
            pl.BlockSpec((tm, d), lambda i: (i, 0)),
            tok, tok, tok, tok,
            pl.BlockSpec((4, GROUP_W, d), lambda i: (0, 0, 0)),
            pl.BlockSpec((1, d), lambda i: (0, 0)),
        ],
        out_specs=pl.BlockSpec((tm, d), lambda i: (i, 0)),
        out_shape=jax.ShapeDtypeStruct((m, d), F32),
        compiler_params=pltpu.CompilerParams(
            dimension_semantics=("parallel",), vmem_limit_bytes=VMEM_LIMIT),
        name="out_proj",
    )(x, o_a, o_b, o_c, o_d, w, g.reshape(1, d))


def _softmax_step(sc, mask, m_ref, l_ref, idx):
    sc = jnp.where(mask, sc, NEG_INF)
    m_old = m_ref[idx]
    m_new = jnp.maximum(m_old, jnp.max(sc, axis=-1, keepdims=True))
    p = jnp.where(mask, jnp.exp(sc - m_new), 0.0)
    alpha = jnp.exp(m_old - m_new)
    l_ref[idx] = alpha * l_ref[idx] + jnp.sum(p, axis=-1, keepdims=True)
    m_ref[idx] = m_new
    return p, alpha


def _dot_nt(a, b):
    return lax.dot_general(a, b, (((1,), (1,)), ((), ())), preferred_element_type=F32)


def _diff_lambda(lv_ref, lam_init):
    lv = lv_ref[...]
    a = jnp.sum(lv[0:1] * lv[1:2], axis=-1, keepdims=True)
    b = jnp.sum(lv[2:3] * lv[3:4], axis=-1, keepdims=True)
    return jnp.exp(a) - jnp.exp(b) + lam_init


def _diff_streams(q, rows):
    lane = lax.broadcasted_iota(jnp.int32, (rows, 128), 1)
    parts = []
    for c in range(2):
        keep = (lane < DIFF_DH) if c == 0 else (lane >= DIFF_DH)
        for h in range(DIFF_HEADS):
            parts.append(jnp.where(keep, q[:, h * 128:(h + 1) * 128], 0.0))
    return jnp.concatenate(parts, axis=0).astype(BF16)


def _diff_prompt_body(lv_ref, q_ref, k_ref, v_ref, g_ref, o_ref, m_ref, l_ref, acc_ref, *, tq, tk, lam_init):
    qi = pl.program_id(1)
    q0 = qi * tq
    qst = _diff_streams(q_ref[...], tq)
    m_ref[...] = jnp.full_like(m_ref, NEG_INF)
    l_ref[...] = jnp.zeros_like(l_ref)
    acc_ref[...] = jnp.zeros_like(acc_ref)
    slopes = [2.0 ** (-8.0 * (h + 1) / DIFF_HEADS) for h in range(DIFF_HEADS)]
    ri = lax.broadcasted_iota(jnp.int32, (tq, tk), 0)
    ci = lax.broadcasted_iota(jnp.int32, (tq, tk), 1)

    def chunk(kc, carry):
        k0 = pl.multiple_of(kc * tk, tk)
        kb = k_ref[pl.ds(k0, tk), :].astype(BF16)
        vb = v_ref[pl.ds(k0, tk), :].astype(BF16)
        s_all = _dot_nt(qst, kb)
        dist = (q0 - k0) + ri - ci
        mask = dist >= 0
        distf = dist.astype(F32)
        ps, alphas = [], []
        for s in range(2 * DIFF_HEADS):
            sc = s_all[s * tq:(s + 1) * tq] * (DIFF_DH ** -0.5) - slopes[s % DIFF_HEADS] * distf
            p, alpha = _softmax_step(sc, mask, m_ref, l_ref, s)
            ps.append(p.astype(BF16))
            alphas.append(alpha)
        pv = jnp.dot(jnp.concatenate(ps, axis=0), vb, preferred_element_type=F32)
        for s in range(2 * DIFF_HEADS):
            acc_ref[s] = alphas[s] * acc_ref[s] + pv[s * tq:(s + 1) * tq]
        return carry

    lax.fori_loop(0, (q0 + tq + tk - 1) // tk, chunk, 0)
    lam = _diff_lambda(lv_ref, lam_init)
    for h in range(DIFF_HEADS):
        o0 = acc_ref[h] / jnp.maximum(l_ref[h], TINY)
        o1 = acc_ref[DIFF_HEADS + h] / jnp.maximum(l_ref[DIFF_HEADS + h], TINY)
        o_ref[:, h * 128:(h + 1) * 128] = _rms(o0 - lam * o1, g_ref[...]) * (1.0 - lam_init)


def _diff_prompt(u, lv, subln_g, *, b, t, lam_init, tq=128, tk=256):
    nq = t // tq
    return pl.pallas_call(
        functools.partial(_diff_prompt_body, tq=tq, tk=tk, lam_init=lam_init),
        grid=(b, nq),
        in_specs=[
            pl.BlockSpec((4, DIFF_DH), lambda bi, qi: (0, 0)),
            pl.BlockSpec((tq, 512), lambda bi, qi: (bi * nq + qi, COL_DQ // 512)),
            pl.BlockSpec((t, 128), lambda bi, qi: (bi, COL_DK // 128)),
            pl.BlockSpec((t, 128), lambda bi, qi: (bi, COL_DV // 128)),
            pl.BlockSpec((1, DIFF_DV), lambda bi, qi: (0, 0)),
        ],
        out_specs=pl.BlockSpec((tq, GROUP_W), lambda bi, qi: (bi * nq + qi, 0)),
        out_shape=jax.ShapeDtypeStruct((b * t, GROUP_W), F32),
        scratch_shapes=[pltpu.VMEM((8, tq, 1), F32), pltpu.VMEM((8, tq, 1), F32),
                        pltpu.VMEM((8, tq, 128), F32)],
        compiler_params=pltpu.CompilerParams(
            dimension_semantics=("parallel", "parallel"), vmem_limit_bytes=VMEM_LIMIT),
        name="diff_prompt",
    )(lv, u, u, u, subln_g.reshape(1, DIFF_DV))


def _rope_tables(pos):
    half = MLA_D_ROPE // 2
    inv = ROPE_THETA ** (-jnp.arange(half, dtype=F32) / half)
    ang = pos.astype(F32)[:, None] * inv
    cos, sin = jnp.cos(ang), jnp.sin(ang)
    cos_t = jnp.tile(cos, (1, 128 // half))
    sin_t = jnp.tile(jnp.concatenate([-sin, sin], axis=-1), (1, 128 // MLA_D_ROPE))
    return cos_t, sin_t


def _rope_lanes(x, cos_t, sin_t):
    lane = lax.broadcasted_iota(jnp.int32, x.shape, 1)
    first = (lane % MLA_D_ROPE) < (MLA_D_ROPE // 2)
    partner = jnp.where(first, pltpu.roll(x, 128 - MLA_D_ROPE // 2, 1), pltpu.roll(x, MLA_D_ROPE // 2, 1))
    return x * cos_t + partner * sin_t


def _mla_weight_layout(w_uq, w_uk, w_uv):
    w = w_uq.reshape(MLA_Q_RANK, MLA_HEADS, MLA_D_NOPE + MLA_D_ROPE)
    wuq = jnp.concatenate([w[:, :, :MLA_D_NOPE].reshape(MLA_Q_RANK, -1),
                           w[:, :, MLA_D_NOPE:].reshape(MLA_Q_RANK, -1)], axis=-1)
    return wuq.astype(BF16), w_uk.transpose(1, 2, 0).astype(BF16), w_uv.transpose(1, 0, 2).astype(BF16)


def _mla_prep_body(cq_ref, ckv_ref, sm_ref, cos_ref, sin_ref, gq_ref, gkv_ref, wuq_ref, wuk_ref, q_ref, k_ref):
    tm = cq_ref.shape[0]
    cos_t, sin_t = cos_ref[...], sin_ref[...]
    lane = lax.broadcasted_iota(jnp.int32, (tm, 128), 1)
    low = lane < MLA_D_ROPE
    k_ref[:, 0:128] = _rms(ckv_ref[...], gkv_ref[...])
    k_ref[:, 128:256] = jnp.where(low, _rope_lanes(sm_ref[...], cos_t, sin_t), 0.0)
    qm = jnp.dot(_rms(cq_ref[...], gq_ref[...]).astype(BF16), wuq_ref[...], preferred_element_type=F32)
    qr = _rope_lanes(qm[:, MLA_HEADS * MLA_D_NOPE:], cos_t, sin_t)
    for h in range(MLA_HEADS):
        qn = qm[:, h * MLA_D_NOPE:(h + 1) * MLA_D_NOPE].astype(BF16)
        q_ref[:, h * 256:h * 256 + 128] = jnp.dot(qn, wuk_ref[h], preferred_element_type=F32).astype(BF16)
        qr_h = qr if h == 0 else pltpu.roll(qr, 128 - h * MLA_D_ROPE, 1)
        q_ref[:, h * 256 + 128:(h + 1) * 256] = jnp.where(low, qr_h, 0.0).astype(BF16)


def _mla_prep(u, cos_t, sin_t, gq, gkv, wuq, wuk, *, tm=544):
    m = u.shape[0]
    row = lambda i: (i, 0)
    fixed2 = lambda i: (0, 0)
    return pl.pallas_call(
        _mla_prep_body,
        grid=(m // tm,),
        in_specs=[
            pl.BlockSpec((tm, MLA_Q_RANK), lambda i: (i, COL_MCQ // MLA_Q_RANK)),
            pl.BlockSpec((tm, 128), lambda i: (i, COL_MCKV // 128)),
            pl.BlockSpec((tm, 128), lambda i: (i, COL_SMALL // 128)),
            pl.BlockSpec((tm, 128), row),
            pl.BlockSpec((tm, 128), row),
            pl.BlockSpec((1, MLA_Q_RANK), fixed2),
            pl.BlockSpec((1, MLA_KV_RANK), fixed2),
            pl.BlockSpec((MLA_Q_RANK, 640), fixed2),
            pl.BlockSpec((MLA_HEADS, 128, 128), lambda i: (0, 0, 0)),
        ],
        out_specs=[pl.BlockSpec((tm, MLA_HEADS * 256), row), pl.BlockSpec((tm, 256), row)],
        out_shape=[jax.ShapeDtypeStruct((m, MLA_HEADS * 256), BF16), jax.ShapeDtypeStruct((m, 256), F32)],
        compiler_params=pltpu.CompilerParams(dimension_semantics=("parallel",), vmem_limit_bytes=VMEM_LIMIT),
        name="mla_prep",
    )(u, u, u, cos_t, sin_t, gq.reshape(1, -1), gkv.reshape(1, -1), wuq, wuk)


def _mla_prompt_body(q_ref, k_ref, wuv_ref, g_ref, o_ref, m_ref, l_ref, acc_ref, *, tq, tk):
    qi = pl.program_id(1)
    q0 = qi * tq
    qst = jnp.concatenate([q_ref[:, h * 256:(h + 1) * 256] for h in range(MLA_HEADS)], axis=0)
    m_ref[...] = jnp.full_like(m_ref, NEG_INF)
    l_ref[...] = jnp.zeros_like(l_ref)
    acc_ref[...] = jnp.zeros_like(acc_ref)
    ri = lax.broadcasted_iota(jnp.int32, (tq, tk), 0)
    ci = lax.broadcasted_iota(jnp.int32, (tq, tk), 1)
    scale = (MLA_D_NOPE + MLA_D_ROPE) ** -0.5

    def chunk(kc, carry):
        k0 = pl.multiple_of(kc * tk, tk)
        kb = k_ref[pl.ds(k0, tk), :].astype(BF16)
        s_all = _dot_nt(qst, kb)
        mask = ((q0 - k0) + ri - ci) >= 0
        ps, alphas = [], []
        for h in range(MLA_HEADS):
            p, alpha = _softmax_step(s_all[h * tq:(h + 1) * tq] * scale, mask, m_ref, l_ref, h)
            ps.append(p.astype(BF16))
            alphas.append(alpha)
        pv = jnp.dot(jnp.concatenate(ps, axis=0), kb[:, 0:MLA_KV_RANK], preferred_element_type=F32)
        for h in range(MLA_HEADS):
            acc_ref[h] = alphas[h] * acc_ref[h] + pv[h * tq:(h + 1) * tq]
        return carry

    lax.fori_loop(0, (q0 + tq + tk - 1) // tk, chunk, 0)
    outs = []
    for h in range(MLA_HEADS):
        o_lat = (acc_ref[h] / jnp.maximum(l_ref[h], TINY)).astype(BF16)
        outs.append(jnp.dot(o_lat, wuv_ref[h], preferred_element_type=F32))
    o_ref[...] = _rms(jnp.concatenate(outs, axis=-1), g_ref[...])


def _mla_prompt(q, k, wuv, g, *, b, t, tq=128, tk=256):
    nq = t // tq
    return pl.pallas_call(
        functools.partial(_mla_prompt_body, tq=tq, tk=tk),
        grid=(b, nq),
        in_specs=[
            pl.BlockSpec((tq, MLA_HEADS * 256), lambda bi, qi: (bi * nq + qi, 0)),
            pl.BlockSpec((t, 256), lambda bi, qi: (bi, 0)),
            pl.BlockSpec((MLA_HEADS, 128, 128), lambda bi, qi: (0, 0, 0)),
            pl.BlockSpec((1, GROUP_W), lambda bi, qi: (0, 0)),
        ],
        out_specs=pl.BlockSpec((tq, GROUP_W), lambda bi, qi: (bi * nq + qi, 0)),
        out_shape=jax.ShapeDtypeStruct((b * t, GROUP_W), F32),
        scratch_shapes=[pltpu.VMEM((MLA_HEADS, tq, 1), F32), pltpu.VMEM((MLA_HEADS, tq, 1), F32),
                        pltpu.VMEM((MLA_HEADS, tq, 128), F32)],
        compiler_params=pltpu.CompilerParams(
            dimension_semantics=("parallel", "parallel"), vmem_limit_bytes=VMEM_LIMIT),
        name="mla_prompt",
    )(q, k, wuv, g.reshape(1, GROUP_W))


def _nsa_slopes():
    return [2.0 ** (-8.0 * (h + 1) / NSA_HEADS) for h in range(NSA_HEADS)]


def _nsa_select(imp, posq, lane):
    cur = lax.shift_right_logical(posq, 6)
    forced = (lane == 0) | (lane == cur) | (lane == cur - 1)
    score = jnp.where(lane <= cur, jnp.where(forced, FORCE_SCORE, imp), -1.0)
    lane_f = lane.astype(F32)
    sel = jnp.zeros(imp.shape, F32)
    for _ in range(NSA_TOPK):
        mx = jnp.max(score, axis=-1, keepdims=True)
        idx = jnp.min(jnp.where(score == mx, lane_f, 1e9), axis=-1, keepdims=True)
        hit = lane_f == idx
        sel = jnp.where(hit & (mx >= 0.0), 1.0, sel)
        score = jnp.where(hit, -2.0, score)
    return sel


def _nsa_prompt_body(q_ref, sm_ref, kv_ref, g_ref, o_ref, kc_ref, vc_ref, ke_ref, ko_ref, vd_ref,
                     m_ref, l_ref, acc_ref, *, t, tq, tk):
    qi = pl.program_id(1)
    q0 = qi * tq
    nb = t // NSA_BLOCK
    slopes = _nsa_slopes()

    @pl.when(qi == 0)
    def _():
        lo_t = lax.broadcasted_iota(jnp.int32, (t, 128), 1) < NSA_DH
        for br in range(2):
            kvb = kv_ref[:, 128 * (br + 1):128 * (br + 2)]
            sw = pltpu.roll(kvb, NSA_DH, 1)
            ke_ref[br] = jnp.where(lo_t, kvb, 0.0).astype(BF16)
            ko_ref[br] = jnp.where(lo_t, 0.0, sw).astype(BF16)
            vd_ref[br] = jnp.where(lo_t, sw, kvb).astype(BF16)
        cm = jnp.sum(kv_ref[:, 0:128].reshape(nb, NSA_BLOCK, 128), axis=1) * (1.0 / NSA_BLOCK)
        cm = jnp.concatenate([cm, jnp.zeros((128 - nb, 128), F32)], axis=0)
        sw = pltpu.roll(cm, NSA_DH, 1)
        lo_c = lax.broadcasted_iota(jnp.int32, (128, 128), 1) < NSA_DH
        kc_ref[0:128, :] = jnp.where(lo_c, cm, 0.0).astype(BF16)
        kc_ref[128:256, :] = jnp.where(lo_c, 0.0, sw).astype(BF16)
        vc_ref[...] = jnp.where(lo_c, sw, cm).astype(BF16)

    q4 = jnp.concatenate([q_ref[:, j * 128:(j + 1) * 128] for j in range(4)], axis=0).astype(BF16)
    lane = lax.broadcasted_iota(jnp.int32, (tq, 128), 1)
    posq = q0 + lax.broadcasted_iota(jnp.int32, (tq, 128), 0)

    s_c = _dot_nt(q4, kc_ref[...])
    dist_c = posq - (NSA_BLOCK * (lane + 1) - 1)
    mask_c = dist_c >= 0
    dist_cf = dist_c.astype(F32)
    imp = jnp.zeros((tq, 128), F32)
    pcs = []
    for h in range(NSA_HEADS):
        j, e = divmod(h, 2)
        sc = s_c[j * tq:(j + 1) * tq, e * 128:(e + 1) * 128] * (NSA_DH ** -0.5) - slopes[h] * dist_cf
        sc = jnp.where(mask_c, sc, NEG_INF)
        ex = jnp.where(mask_c, jnp.exp(sc - jnp.max(sc, axis=-1, keepdims=True)), 0.0)
        pc = ex / jnp.maximum(jnp.sum(ex, axis=-1, keepdims=True), TINY)
        imp = imp + pc
        pcs.append(pc.astype(BF16))
    oc = jnp.dot(jnp.concatenate(pcs, axis=0), vc_ref[...], preferred_element_type=F32)
    sel = _nsa_select(imp, posq, lane).astype(BF16)

    m_ref[...] = jnp.full_like(m_ref, NEG_INF)
    l_ref[...] = jnp.zeros_like(l_ref)
    acc_ref[...] = jnp.zeros_like(acc_ref)
    ri = lax.broadcasted_iota(jnp.int32, (tq, tk), 0)
    ci = lax.broadcasted_iota(jnp.int32, (tq, tk), 1)
    bj = lax.broadcasted_iota(jnp.int32, (128, tk), 0)
    bk = lax.broadcasted_iota(jnp.int32, (128, tk), 1)

    def make_chunk(br):
        def chunk(kc, carry):
            k0 = pl.multiple_of(kc * tk, tk)
            s_e = _dot_nt(q4, ke_ref[br, pl.ds(k0, tk), :])
            s_o = _dot_nt(q4, ko_ref[br, pl.ds(k0, tk), :])
            dist = (q0 - k0) + ri - ci
            distf = dist.astype(F32)
            if br == 0:
                expand = jnp.where(lax.shift_right_logical(k0 + bk, 6) == bj, 1.0, 0.0).astype(BF16)
                selk = jnp.dot(sel, expand, preferred_element_type=F32)
                mask = jnp.where(dist >= 0, selk, 0.0) > 0.5
            else:
                mask = jnp.where(dist >= 0, dist, NSA_WINDOW) < NSA_WINDOW
            ps, alphas = [], []
            for h in range(NSA_HEADS):
                j, e = divmod(h, 2)
                sc = (s_e if e == 0 else s_o)[j * tq:(j + 1) * tq] * (NSA_DH ** -0.5) - slopes[h] * distf
                p, alpha = _softmax_step(sc, mask, m_ref, l_ref, br * NSA_HEADS + h)
                ps.append(p.astype(BF16))
                alphas.append(alpha)
            pv = jnp.dot(jnp.concatenate(ps, axis=0), vd_ref[br, pl.ds(k0, tk), :], preferred_element_type=F32)
            for h in range(NSA_HEADS):
                i = br * NSA_HEADS + h
                acc_ref[i] = alphas[h] * acc_ref[i] + pv[h * tq:(h + 1) * tq]
            return carry
        return chunk

    n_chunks = (q0 + tq + tk - 1) // tk
    lax.fori_loop(0, n_chunks, make_chunk(0), 0)
    lax.fori_loop(jnp.maximum(q0 - (NSA_WINDOW - 1), 0) // tk, n_chunks, make_chunk(1), 0)

    gs = jax.nn.sigmoid(sm_ref[...])
    outs = []
    for j in range(4):
        pair = []
        for e in range(2):
            h = 2 * j + e
            c0 = MLA_D_ROPE + 3 * h
            o_s = acc_ref[h] / jnp.maximum(l_ref[h], TINY)
            o_w = acc_ref[NSA_HEADS + h] / jnp.maximum(l_ref[NSA_HEADS + h], TINY)
            pair.append(gs[:, c0:c0 + 1] * oc[h * tq:(h + 1) * tq] + gs[:, c0 + 1:c0 + 2] * o_s
                        + gs[:, c0 + 2:c0 + 3] * o_w)
        outs.append(jnp.where(lane < NSA_DH, pair[0], pair[1]))
    o_ref[...] = _rms(jnp.concatenate(outs, axis=-1), g_ref[...])


def _nsa_prompt(u, g, *, b, t, tq=128, tk=256):
    nq = t // tq
    return pl.pallas_call(
        functools.partial(_nsa_prompt_body, t=t, tq=tq, tk=tk),
        grid=(b, nq),
        in_specs=[
            pl.BlockSpec((tq, 512), lambda bi, qi: (bi * nq + qi, COL_NQ // 512)),
            pl.BlockSpec((tq, 128), lambda bi, qi: (bi * nq + qi, COL_SMALL // 128)),
            pl.BlockSpec((t, 384), lambda bi, qi: (bi, COL_NKV // 384)),
            pl.BlockSpec((1, GROUP_W), lambda bi, qi: (0, 0)),
        ],
        out_specs=pl.BlockSpec((tq, GROUP_W), lambda bi, qi: (bi * nq + qi, 0)),
        out_shape=jax.ShapeDtypeStruct((b * t, GROUP_W), F32),
        scratch_shapes=[pltpu.VMEM((256, 128), BF16), pltpu.VMEM((128, 128), BF16),
                        pltpu.VMEM((2, t, 128), BF16), pltpu.VMEM((2, t, 128), BF16), pltpu.VMEM((2, t, 128), BF16),
                        pltpu.VMEM((2 * NSA_HEADS, tq, 1), F32), pltpu.VMEM((2 * NSA_HEADS, tq, 1), F32),
                        pltpu.VMEM((2 * NSA_HEADS, tq, 128), F32)],
        compiler_params=pltpu.CompilerParams(
            dimension_semantics=("parallel", "arbitrary"), vmem_limit_bytes=VMEM_LIMIT),
        name="nsa_prompt",
    )(u, u, u, g.reshape(1, GROUP_W))


TS_PAD = 8
PAGES_PER_STEP = 16


def _page_specs(layer, width, pps):
    def spec(j):
        return pl.BlockSpec((None, None, 128, width), lambda b, g, pt: (layer, pt[b, g * pps + j], 0, 0))
    return [spec(j) for j in range(pps)]


def _pad_keys(x):
    return jnp.concatenate([x, jnp.zeros((128 - TS_PAD, x.shape[1]), x.dtype)], axis=0)


def _diff_decode_body(pt_ref, lv_ref, q_ref, new_ref, g_ref, *rest, pps, ts, past_len, lam_init):
    pages = rest[:pps]
    o_ref, qst_ref, m_ref, l_ref, acc_ref = rest[pps:]
    g = pl.program_id(1)
    rows = 2 * DIFF_HEADS * TS_PAD
    w = pps * 128

    @pl.when(g == 0)
    def _():
        qst_ref[...] = _diff_streams(q_ref[...], TS_PAD)
        m_ref[...] = jnp.full_like(m_ref, NEG_INF)
        l_ref[...] = jnp.zeros_like(l_ref)
        acc_ref[...] = jnp.zeros_like(acc_ref)

    qst = qst_ref[...]
    ri = lax.broadcasted_iota(jnp.int32, (rows, 1), 0)
    head = (ri >> 3) & (DIFF_HEADS - 1)
    sl = [2.0 ** (-8.0 * (h + 1) / DIFF_HEADS) for h in range(DIFF_HEADS)]
    slope = jnp.where(head == 0, sl[0], jnp.where(head == 1, sl[1], jnp.where(head == 2, sl[2], sl[3])))
    tok = ri & (TS_PAD - 1)

    def update(sc, mask, vals):
        if mask is not None:
            sc = jnp.where(mask, sc, NEG_INF)
        m_old = m_ref[...]
        m_new = jnp.maximum(m_old, jnp.max(sc, axis=-1, keepdims=True))
        p = jnp.exp(sc - m_new)
        if mask is not None:
            p = jnp.where(mask, p, 0.0)
        alpha = jnp.exp(m_old - m_new)
        l_ref[...] = alpha * l_ref[...] + jnp.sum(p, axis=-1, keepdims=True)
        m_ref[...] = m_new
        acc_ref[...] = alpha * acc_ref[...] + jnp.dot(p.astype(BF16), vals, preferred_element_type=F32)

    kcat = jnp.concatenate([pages[j][:, 0:128].astype(BF16) for j in range(pps)], axis=0)
    vcat = jnp.concatenate([pages[j][:, 128:256].astype(BF16) for j in range(pps)], axis=0)
    ci = lax.broadcasted_iota(jnp.int32, (rows, w), 1)
    distf = ((past_len - g * w) + tok - ci).astype(F32)
    update(_dot_nt(qst, kcat) * (DIFF_DH ** -0.5) - slope * distf, None, vcat)

    @pl.when(g == pl.num_programs(1) - 1)
    def _():
        new = _pad_keys(new_ref[...])
        sn = _dot_nt(qst, new[:, 0:128].astype(BF16))
        cn = lax.broadcasted_iota(jnp.int32, (rows, 128), 1)
        dist = tok - cn
        mask = (dist >= 0) & (cn < ts)
        update(sn * (DIFF_DH ** -0.5) - slope * dist.astype(F32), mask, new[:, 128:256].astype(BF16))
        lam = _diff_lambda(lv_ref, lam_init)
        acc = acc_ref[...] / jnp.maximum(l_ref[...], TINY)
        for h in range(DIFF_HEADS):
            o0 = acc[h * TS_PAD:(h + 1) * TS_PAD]
            o1 = acc[(DIFF_HEADS + h) * TS_PAD:(DIFF_HEADS + h + 1) * TS_PAD]
            o_ref[:, h * 128:(h + 1) * 128] = _rms(o0 - lam * o1, g_ref[...]) * (1.0 - lam_init)


def _diff_decode(page_table, u_s, cache, lv, subln_g, *, layer, ts, lam_init, pps=PAGES_PER_STEP):
    bs, n_pages = page_table.shape
    rows = 2 * DIFF_HEADS * TS_PAD
    grid_spec = pltpu.PrefetchScalarGridSpec(
        num_scalar_prefetch=1,
        grid=(bs, n_pages // pps),
        in_specs=[
            pl.BlockSpec((4, DIFF_DH), lambda b, g, pt: (0, 0)),
            pl.BlockSpec((None, TS_PAD, 512), lambda b, g, pt: (b, 0, COL_DQ // 512)),
            pl.BlockSpec((None, TS_PAD, 256), lambda b, g, pt: (b, 0, COL_DK // 256)),
            pl.BlockSpec((1, DIFF_DV), lambda b, g, pt: (0, 0)),
        ] + _page_specs(layer, 256, pps),
        out_specs=pl.BlockSpec((None, TS_PAD, GROUP_W), lambda b, g, pt: (b, 0, 0)),
        scratch_shapes=[pltpu.VMEM((rows, 128), BF16), pltpu.VMEM((rows, 1), F32), pltpu.VMEM((rows, 1), F32),
                        pltpu.VMEM((rows, 128), F32)],
    )
    return pl.pallas_call(
        functools.partial(_diff_decode_body, pps=pps, ts=ts, past_len=n_pages * 128, lam_init=lam_init),
        grid_spec=grid_spec,
        out_shape=jax.ShapeDtypeStruct((bs, TS_PAD, GROUP_W), F32),
        compiler_params=pltpu.CompilerParams(
            dimension_semantics=("parallel", "arbitrary"), vmem_limit_bytes=VMEM_LIMIT),
        name="diff_decode",
    )(page_table, lv, u_s, u_s, subln_g.reshape(1, DIFF_DV), *([cache] * pps))


def _mla_decode_body(pt_ref, q_ref, new_ref, wuv_ref, g_ref, *rest, pps, ts):
    pages = rest[:pps]
    o_ref, qst_ref, m_ref, l_ref, acc_ref = rest[pps:]
    g = pl.program_id(1)
    rows = MLA_HEADS * TS_PAD
    scale = (MLA_D_NOPE + MLA_D_ROPE) ** -0.5

    @pl.when(g == 0)
    def _():
        qst_ref[...] = jnp.concatenate([q_ref[:, h * 256:(h + 1) * 256] for h in range(MLA_HEADS)], axis=0)
        m_ref[...] = jnp.full_like(m_ref, NEG_INF)
        l_ref[...] = jnp.zeros_like(l_ref)
        acc_ref[...] = jnp.zeros_like(acc_ref)

    q_lat = qst_ref[:, 0:MLA_KV_RANK]
    q_rope = qst_ref[:, MLA_KV_RANK:MLA_KV_RANK + MLA_D_ROPE]

    def scores(lat, kr):
        return (_dot_nt(q_lat, lat) + _dot_nt(q_rope, kr)) * scale

    def update(sc, mask, vals):
        if mask is not None:
            sc = jnp.where(mask, sc, NEG_INF)
        m_old = m_ref[...]
        m_new = jnp.maximum(m_old, jnp.max(sc, axis=-1, keepdims=True))
        p = jnp.exp(sc - m_new)
        if mask is not None:
            p = jnp.where(mask, p, 0.0)
        alpha = jnp.exp(m_old - m_new)
        l_ref[...] = alpha * l_ref[...] + jnp.sum(p, axis=-1, keepdims=True)
        m_ref[...] = m_new
        acc_ref[...] = alpha * acc_ref[...] + jnp.dot(p.astype(BF16), vals, preferred_element_type=F32)

    lat_cat = jnp.concatenate([pages[j][:, 0:MLA_KV_RANK].astype(BF16) for j in range(pps)], axis=0)
    kr_cat = jnp.concatenate(
        [pages[j][:, MLA_KV_RANK:MLA_KV_RANK + MLA_D_ROPE].astype(BF16) for j in range(pps)], axis=0)
    update(scores(lat_cat, kr_cat), None, lat_cat)

    @pl.when(g == pl.num_programs(1) - 1)
    def _():
        new = _pad_keys(new_ref[...])
        lat = new[:, 0:MLA_KV_RANK].astype(BF16)
        sn = scores(lat, new[:, MLA_KV_RANK:MLA_KV_RANK + MLA_D_ROPE].astype(BF16))
        tok = lax.broadcasted_iota(jnp.int32, (rows, 128), 0) & (TS_PAD - 1)
        cn = lax.broadcasted_iota(jnp.int32, (rows, 128), 1)
        update(sn, (cn <= tok) & (cn < ts), lat)
        acc = acc_ref[...] / jnp.maximum(l_ref[...], TINY)
        outs = [jnp.dot(acc[h * TS_PAD:(h + 1) * TS_PAD].astype(BF16), wuv_ref[h], preferred_element_type=F32)
                for h in range(MLA_HEADS)]
        o_ref[...] = _rms(jnp.concatenate(outs, axis=-1), g_ref[...])


def _mla_decode(page_table, q_s, k_s, cache, wuv, g, *, layer, ts, pps=PAGES_PER_STEP):
    bs, n_pages = page_table.shape
    rows = MLA_HEADS * TS_PAD
    width = cache.shape[-1]
    grid_spec = pltpu.PrefetchScalarGridSpec(
        num_scalar_prefetch=1,
        grid=(bs, n_pages // pps),
        in_specs=[
            pl.BlockSpec((None, TS_PAD, MLA_HEADS * 256), lambda b, g, pt: (b, 0, 0)),
            pl.BlockSpec((None, TS_PAD, 256), lambda b, g, pt: (b, 0, 0)),
            pl.BlockSpec((MLA_HEADS, 128, 128), lambda b, g, pt: (0, 0, 0)),
            pl.BlockSpec((1, GROUP_W), lambda b, g, pt: (0, 0)),
        ] + _page_specs(layer, width, pps),
        out_specs=pl.BlockSpec((None, TS_PAD, GROUP_W), lambda b, g, pt: (b, 0, 0)),
        scratch_shapes=[pltpu.VMEM((rows, 256), BF16), pltpu.VMEM((rows, 1), F32), pltpu.VMEM((rows, 1), F32),
                        pltpu.VMEM((rows, 128), F32)],
    )
    return pl.pallas_call(
        functools.partial(_mla_decode_body, pps=pps, ts=ts),
        grid_spec=grid_spec,
        out_shape=jax.ShapeDtypeStruct((bs, TS_PAD, GROUP_W), F32),
        compiler_params=pltpu.CompilerParams(
            dimension_semantics=("parallel", "arbitrary"), vmem_limit_bytes=VMEM_LIMIT),
        name="mla_decode",
    )(page_table, q_s, k_s, wuv, g.reshape(1, GROUP_W), *([cache] * pps))


def _nsa_decode_body(pt_ref, q_ref, sm_ref, new_ref, win_ref, g_ref, *rest, pps, ts, past_len, w_buf):
    pages = rest[:pps]
    o_ref, cm_ref, kv_ref, selm_ref, m_ref, l_ref, acc_ref = rest[pps:]
    g = pl.program_id(1)
    ck = pps * 128
    bpc = ck // NSA_BLOCK
    n_chunks = past_len // ck
    nb_past = past_len // NSA_BLOCK
    slopes = _nsa_slopes()
    scale = NSA_DH ** -0.5

    cmp_rows = jnp.concatenate([pages[j][:, 0:128] for j in range(pps)], axis=0)
    cm_ref[pl.ds(pl.multiple_of(g * bpc, bpc), bpc), :] = (
        jnp.sum(cmp_rows.reshape(bpc, NSA_BLOCK, 128), axis=1) * (1.0 / NSA_BLOCK))
    for j in range(pps):
        kv_ref[pl.ds(pl.multiple_of(g * ck + j * 128, 128), 128), :] = pages[j][:, 128:256].astype(BF16)

    @pl.when(g == pl.num_programs(1) - 1)
    def _():
        q4 = jnp.concatenate([q_ref[:, j * 128:(j + 1) * 128] for j in range(4)], axis=0)
        lo_q = lax.broadcasted_iota(jnp.int32, q4.shape, 1) < NSA_DH
        q_e = jnp.where(lo_q, q4, 0.0).astype(BF16)
        q_o = jnp.where(lo_q, pltpu.roll(q4, NSA_DH, 1), 0.0).astype(BF16)
        m_ref[...] = jnp.full_like(m_ref, NEG_INF)
        l_ref[...] = jnp.zeros_like(l_ref)
        acc_ref[...] = jnp.zeros_like(acc_ref)

        def attend(br, kv, distf, mask):
            s_e, s_o = _dot_nt(q_e, kv), _dot_nt(q_o, kv)
            vd = kv
            ps, alphas = [], []
            for h in range(NSA_HEADS):
                j, e = divmod(h, 2)
                sc = (s_e if e == 0 else s_o)[j * TS_PAD:(j + 1) * TS_PAD] * scale - slopes[h] * distf
                p, alpha = _softmax_step(sc, mask, m_ref, l_ref, br * NSA_HEADS + h)
                ps.append(p.astype(BF16))
                alphas.append(alpha)
            pv = jnp.dot(jnp.concatenate(ps, axis=0), vd, preferred_element_type=F32)
            for h in range(NSA_HEADS):
                i = br * NSA_HEADS + h
                acc_ref[i] = alphas[h] * acc_ref[i] + pv[h * TS_PAD:(h + 1) * TS_PAD]

        cmb = cm_ref[...].astype(BF16)
        s_e, s_o = _dot_nt(q_e, cmb), _dot_nt(q_o, cmb)
        blk = lax.broadcasted_iota(jnp.int32, (TS_PAD, nb_past), 1)
        tok_c = lax.broadcasted_iota(jnp.int32, (TS_PAD, nb_past), 0)
        dist_c = (past_len + tok_c) - (NSA_BLOCK * (blk + 1) - 1)
        mask_c = dist_c >= 0
        dist_cf = dist_c.astype(F32)
        imp = jnp.zeros((TS_PAD, nb_past), F32)
        pcs = []
        for h in range(NSA_HEADS):
            j, e = divmod(h, 2)
            sc = (s_e if e == 0 else s_o)[j * TS_PAD:(j + 1) * TS_PAD] * scale - slopes[h] * dist_cf
            sc = jnp.where(mask_c, sc, NEG_INF)
            ex = jnp.where(mask_c, jnp.exp(sc - jnp.max(sc, axis=-1, keepdims=True)), 0.0)
            pc = ex / jnp.maximum(jnp.sum(ex, axis=-1, keepdims=True), TINY)
            imp = imp + pc
            pcs.append(pc.astype(BF16))
        oc = jnp.dot(jnp.concatenate(pcs, axis=0), cmb, preferred_element_type=F32)

        nbl = nb_past + 128
        lane_b = lax.broadcasted_iota(jnp.int32, (TS_PAD, nbl), 1)
        posq = past_len + lax.broadcasted_iota(jnp.int32, (TS_PAD, nbl), 0)
        sel = _nsa_select(jnp.concatenate([imp, jnp.zeros((TS_PAD, 128), F32)], axis=1), posq, lane_b)
        for c in range(n_chunks):
            selm_ref[c] = sel[:, c * bpc:(c + 1) * bpc]

        tok = lax.broadcasted_iota(jnp.int32, (TS_PAD, ck), 0)
        ci = lax.broadcasted_iota(jnp.int32, (TS_PAD, ck), 1)
        expand = jnp.where(
            lax.shift_right_logical(lax.broadcasted_iota(jnp.int32, (bpc, ck), 1), 6)
            == lax.broadcasted_iota(jnp.int32, (bpc, ck), 0), 1.0, 0.0).astype(BF16)

        def chunk(c, carry):
            k0 = pl.multiple_of(c * ck, ck)
            selk = jnp.dot(selm_ref[c].astype(BF16), expand, preferred_element_type=F32)
            distf = ((past_len - k0) + tok - ci).astype(F32)
            attend(0, kv_ref[pl.ds(k0, ck), :], distf, selk > 0.5)
            return carry

        lax.fori_loop(0, n_chunks, chunk, 0)

        tok_n = lax.broadcasted_iota(jnp.int32, (TS_PAD, 128), 0)
        cn = lax.broadcasted_iota(jnp.int32, (TS_PAD, 128), 1)
        dist_n = tok_n - cn
        causal_n = (dist_n >= 0) & (cn < ts)
        dist_nf = dist_n.astype(F32)
        new = _pad_keys(new_ref[...])
        attend(0, new[:, 128:256].astype(BF16), dist_nf, causal_n & (sel[:, nb_past:nb_past + 1] > 0.5))
        attend(1, new[:, 256:384].astype(BF16), dist_nf, causal_n)

        dist_w = (lax.broadcasted_iota(jnp.int32, (TS_PAD, w_buf), 0) + w_buf
                  - lax.broadcasted_iota(jnp.int32, (TS_PAD, w_buf), 1))
        attend(1, win_ref[...].astype(BF16), dist_w.astype(F32), dist_w < NSA_WINDOW)

        gs = jax.nn.sigmoid(sm_ref[...])
        outs = []
        for j in range(4):
            pair = []
            for e in range(2):
                h = 2 * j + e
                c0 = MLA_D_ROPE + 3 * h
                o_s = acc_ref[h] / jnp.maximum(l_ref[h], TINY)
                o_w = acc_ref[NSA_HEADS + h] / jnp.maximum(l_ref[NSA_HEADS + h], TINY)
                pair.append(gs[:, c0:c0 + 1] * oc[h * TS_PAD:(h + 1) * TS_PAD] + gs[:, c0 + 1:c0 + 2] * o_s
                            + gs[:, c0 + 2:c0 + 3] * o_w)
            outs.append(jnp.where(cn < NSA_DH, pltpu.roll(pair[0], NSA_DH, 1), pair[1]))
        o_ref[...] = _rms(jnp.concatenate(outs, axis=-1), g_ref[...])


def _nsa_decode(page_table, u_s, cache, cache_win, g, *, layer, ts, pps=PAGES_PER_STEP):
    bs, n_pages = page_table.shape
    past_len = n_pages * 128
    w_buf = cache_win.shape[2]
    assert ts < NSA_BLOCK and past_len % (pps * 128) == 0 and w_buf == NSA_WINDOW
    nst = 2 * NSA_HEADS
    grid_spec = pltpu.PrefetchScalarGridSpec(
        num_scalar_prefetch=1,
        grid=(bs, n_pages // pps),
        in_specs=[
            pl.BlockSpec((None, TS_PAD, 512), lambda b, g, pt: (b, 0, COL_NQ // 512)),
            pl.BlockSpec((None, TS_PAD, 128), lambda b, g, pt: (b, 0, COL_SMALL // 128)),
            pl.BlockSpec((None, TS_PAD, 384), lambda b, g, pt: (b, 0, COL_NKV // 384)),
            pl.BlockSpec((None, None, w_buf, 128), lambda b, g, pt: (layer, b, 0, 0)),
            pl.BlockSpec((1, GROUP_W), lambda b, g, pt: (0, 0)),
        ] + _page_specs(layer, 256, pps),
        out_specs=pl.BlockSpec((None, TS_PAD, GROUP_W), lambda b, g, pt: (b, 0, 0)),
        scratch_shapes=[pltpu.VMEM((past_len // NSA_BLOCK, 128), F32),
                        pltpu.VMEM((past_len, 128), BF16),
                        pltpu.VMEM((n_pages // pps, TS_PAD, pps * 128 // NSA_BLOCK), F32),
                        pltpu.VMEM((nst, TS_PAD, 1), F32), pltpu.VMEM((nst, TS_PAD, 1), F32),
                        pltpu.VMEM((nst, TS_PAD, 128), F32)],
    )
    return pl.pallas_call(
        functools.partial(_nsa_decode_body, pps=pps, ts=ts, past_len=past_len, w_buf=w_buf),
        grid_spec=grid_spec,
        out_shape=jax.ShapeDtypeStruct((bs, TS_PAD, GROUP_W), F32),
        compiler_params=pltpu.CompilerParams(
            dimension_semantics=("parallel", "arbitrary"), vmem_limit_bytes=VMEM_LIMIT),
        name="nsa_decode",
    )(page_table, u_s, u_s, u_s, cache_win, g.reshape(1, GROUP_W), *([cache] * pps))


SSD_DT_LANE = MLA_D_ROPE + 3 * NSA_HEADS


def _ssd_body(z_ref, x_ref, sm_ref, tail0_ref, h0_ref, cw_ref, cb_ref, dtb_ref, al_ref, dsk_ref, g_ref,
              o_ref, h_ref, tail_ref, *, q, rows, n_valid):
    @pl.when(pl.program_id(1) == 0)
    def _():
        tail_ref[...] = tail0_ref[...]
        h_ref[...] = h0_ref[...]

    def padded(ref):
        v = ref[...]
        return v if rows == q else jnp.concatenate([v, jnp.zeros((q - rows, v.shape[1]), v.dtype)], axis=0)

    x = padded(x_ref)
    xcat = jnp.concatenate([tail_ref[...], x], axis=0)
    conv = cb_ref[...] + cw_ref[3:4, :] * x
    for k in range(1, SSD_CONV):
        conv = conv + cw_ref[3 - k:4 - k, :] * pltpu.roll(xcat, k, 0)[8:]
    tail_ref[...] = x[q - 8:q]
    xbc = conv * jax.nn.sigmoid(conv)
    xs = xbc[:, 0:GROUP_W]
    xs_b = xs.astype(BF16)
    b_b = xbc[:, GROUP_W:GROUP_W + 256].astype(BF16)
    c_b = xbc[:, GROUP_W + 256:GROUP_W + 512].astype(BF16)

    lane = lax.broadcasted_iota(jnp.int32, (q, 128), 1)
    row = lax.broadcasted_iota(jnp.int32, (q, 128), 0)
    xdt = padded(sm_ref) + dtb_ref[...]
    dt = jnp.maximum(xdt, 0.0) + jnp.log1p(jnp.exp(-jnp.abs(xdt)))
    if n_valid < q:
        dt = jnp.where(row < n_valid, dt, 0.0)
    is_dt = (lane >= SSD_DT_LANE) & (lane < SSD_DT_LANE + SSD_HEADS)
    acum = jnp.where(is_dt, dt * -jnp.exp(al_ref[...]), 0.0)
    s = 1
    while s < q:
        acum = acum + jnp.where(row >= s, pltpu.roll(acum, s, 0), 0.0)
        s *= 2
    acum_t = acum.T
    dt_t = dt.T
    xs_t = xs.T
    causal = lax.broadcasted_iota(jnp.int32, (q, q), 0) >= lax.broadcasted_iota(jnp.int32, (q, q), 1)
    top = lax.broadcasted_iota(jnp.int32, (128, 128), 0) < SSD_HEADDIM
    top_q = lax.broadcasted_iota(jnp.int32, (128, q), 0) < SSD_HEADDIM
    ys = []
    for j in range(SSD_HEADS // 2):
        gi = (2 * j) // (SSD_HEADS // SSD_GROUPS)
        cg = c_b[:, gi * 128:(gi + 1) * 128]
        bg = b_b[:, gi * 128:(gi + 1) * 128]
        cb = _dot_nt(cg, bg)
        xpair = xs_b[:, j * 128:(j + 1) * 128]
        hp = h_ref[j]
        inter = _dot_nt(cg, hp.astype(BF16))
        ypair, to_end, last = [], [], []
        for e in range(2):
            ln = SSD_DT_LANE + 2 * j + e
            col = acum[:, ln:ln + 1]
            rowv = acum_t[ln:ln + 1, :]
            dtr = dt_t[ln:ln + 1, :]
            decay = jnp.where(causal, jnp.exp(jnp.where(causal, col - rowv, 0.0)), 0.0)
            w = (cb * decay * dtr).astype(BF16)
            ypair.append(jnp.dot(w, xpair, preferred_element_type=F32) + inter * jnp.exp(col))
            a_last = rowv[:, q - 1:q]
            to_end.append(jnp.exp(a_last - rowv) * dtr)
            last.append(jnp.exp(a_last))
        ys.append(jnp.where(lane < SSD_HEADDIM, ypair[0], ypair[1]))
        xw = (xs_t[j * 128:(j + 1) * 128, :] * jnp.where(top_q, to_end[0], to_end[1])).astype(BF16)
        h_ref[j] = hp * jnp.where(top, last[0], last[1]) + jnp.dot(xw, bg, preferred_element_type=F32)
    y = (jnp.concatenate(ys, axis=-1) + dsk_ref[...] * xs)[0:rows]
    zz = z_ref[...]
    o_ref[...] = _rms(y * (zz * jax.nn.sigmoid(zz)), g_ref[...])


def _ssd_params(conv_b, dt_bias, a_log, d_skip, norm_g):
    lanes = lambda v: jnp.zeros((1, 128), F32).at[0, SSD_DT_LANE:SSD_DT_LANE + SSD_HEADS].set(v)
    return (conv_b.reshape(1, -1), lanes(dt_bias), lanes(a_log),
            jnp.repeat(d_skip, SSD_HEADDIM).reshape(1, GROUP_W), norm_g.reshape(1, GROUP_W))


def _ssd_param_specs(fixed):
    return [pl.BlockSpec((SSD_CONV, SSD_CONV_CH), fixed), pl.BlockSpec((1, SSD_CONV_CH), fixed),
            pl.BlockSpec((1, 128), fixed), pl.BlockSpec((1, 128), fixed),
            pl.BlockSpec((1, GROUP_W), fixed), pl.BlockSpec((1, GROUP_W), fixed)]


def _ssd_decode(u_s, conv_state, h0, conv_w, conv_b, dt_bias, a_log, d_skip, norm_g, *, layer, ts, q=SSD_CHUNK):
    bs = u_s.shape[0]
    fixed = lambda bi, ci: (0, 0)
    o, h = pl.pallas_call(
        functools.partial(_ssd_body, q=q, rows=TS_PAD, n_valid=ts),
        grid=(bs, 1),
        in_specs=[
            pl.BlockSpec((None, TS_PAD, GROUP_W), lambda bi, ci: (bi, 0, COL_SZ // GROUP_W)),
            pl.BlockSpec((None, TS_PAD, SSD_CONV_CH), lambda bi, ci: (bi, 0, COL_SXBC // SSD_CONV_CH)),
            pl.BlockSpec((None, TS_PAD, 128), lambda bi, ci: (bi, 0, COL_SMALL // 128)),
            pl.BlockSpec((None, 8, SSD_CONV_CH), lambda bi, ci: (bi, 0, 0)),
            pl.BlockSpec((None, None, SSD_HEADS // 2, 128, 128), lambda bi, ci: (layer, bi, 0, 0, 0)),
        ] + _ssd_param_specs(fixed),
        out_specs=[pl.BlockSpec((None, TS_PAD, GROUP_W), lambda bi, ci: (bi, 0, 0)),
                   pl.BlockSpec((None, SSD_HEADS // 2, 128, 128), lambda bi, ci: (bi, 0, 0, 0))],
        out_shape=[jax.ShapeDtypeStruct((bs, TS_PAD, GROUP_W), F32),
                   jax.ShapeDtypeStruct((bs, SSD_HEADS // 2, 128, 128), F32)],
        scratch_shapes=[pltpu.VMEM((8, SSD_CONV_CH), F32)],
        compiler_params=pltpu.CompilerParams(
            dimension_semantics=("parallel", "arbitrary"), vmem_limit_bytes=VMEM_LIMIT),
        name="ssd_decode",
    )(u_s, u_s, u_s, conv_state, h0, conv_w, *_ssd_params(conv_b, dt_bias, a_log, d_skip, norm_g))
    return o, h.reshape(bs, SSD_HEADS, SSD_HEADDIM, SSD_STATE)


def _ssd_prompt(u, conv_w, conv_b, dt_bias, a_log, d_skip, norm_g, *, b, t, q=SSD_CHUNK):
    nc = t // q
    fixed = lambda bi, ci: (0, 0)
    o, h = pl.pallas_call(
        functools.partial(_ssd_body, q=q, rows=q, n_valid=q),
        grid=(b, nc),
        in_specs=[
            pl.BlockSpec((q, GROUP_W), lambda bi, ci: (bi * nc + ci, COL_SZ // GROUP_W)),
            pl.BlockSpec((q, SSD_CONV_CH), lambda bi, ci: (bi * nc + ci, COL_SXBC // SSD_CONV_CH)),
            pl.BlockSpec((q, 128), lambda bi, ci: (bi * nc + ci, COL_SMALL // 128)),
            pl.BlockSpec((8, SSD_CONV_CH), fixed),
            pl.BlockSpec((SSD_HEADS // 2, 128, 128), lambda bi, ci: (0, 0, 0)),
            pl.BlockSpec((SSD_CONV, SSD_CONV_CH), fixed),
            pl.BlockSpec((1, SSD_CONV_CH), fixed),
            pl.BlockSpec((1, 128), fixed),
            pl.BlockSpec((1, 128), fixed),
            pl.BlockSpec((1, GROUP_W), fixed),
            pl.BlockSpec((1, GROUP_W), fixed),
        ],
        out_specs=[pl.BlockSpec((q, GROUP_W), lambda bi, ci: (bi * nc + ci, 0)),
                   pl.BlockSpec((None, SSD_HEADS // 2, 128, 128), lambda bi, ci: (bi, 0, 0, 0))],
        out_shape=[jax.ShapeDtypeStruct((b * t, GROUP_W), F32),
                   jax.ShapeDtypeStruct((b, SSD_HEADS // 2, 128, 128), F32)],
        scratch_shapes=[pltpu.VMEM((8, SSD_CONV_CH), F32)],
        compiler_params=pltpu.CompilerParams(
            dimension_semantics=("parallel", "arbitrary"), vmem_limit_bytes=VMEM_LIMIT),
        name="ssd_prompt",
    )(u, u, u, jnp.zeros((8, SSD_CONV_CH), F32), jnp.zeros((SSD_HEADS // 2, 128, 128), F32), conv_w,
      *_ssd_params(conv_b, dt_bias, a_log, d_skip, norm_g))
    return o, h.reshape(b, SSD_HEADS, SSD_HEADDIM, SSD_STATE)


def _permute_w_in(w_in):
    def sl(a, b):
        return w_in[..., a:b]
    z64 = jnp.zeros(w_in.shape[:-1] + (64,), w_in.dtype)
    return jnp.concatenate([
        sl(0, 512), sl(1312, 1824), sl(2232, 2744), sl(512, 640), sl(640, 768), sl(1152, 1280),
        sl(1280, 1312), sl(2208, 2232), sl(3768, 3776), z64,
        sl(2744, 3768), sl(768, 1152), sl(1824, 2208)], axis=-1)


def kernel(x_prompt, x_sample, cache_diff_kv, cache_mla, cache_nsa_kv, cache_nsa_win, state_ssd_conv, state_ssd_h, page_table, ffn1_pre_g, ffn1_post_g, ffn1_w1, ffn1_w2, mix_pre_g, mix_post_g, w_in, w_out, diff_lambda, diff_subln_g, mla_q_norm_g, mla_w_uq, mla_kv_norm_g, mla_w_uk, mla_w_uv, mla_out_g, nsa_out_g, ssd_conv_w, ssd_conv_b, ssd_dt_bias, ssd_a_log, ssd_d, ssd_norm_g, ffn2_pre_g, ffn2_post_g, ffn2_w1, ffn2_w2):
    depth = w_in.shape[0]
    bp, tp, d = x_prompt.shape
    bs, ts, _ = x_sample.shape
    mp, ms = bp * tp, bs * ts
    past_len = page_table.shape[1] * cache_diff_kv.shape[2]
    w_buf = cache_nsa_win.shape[2]

    w_in_p = _permute_w_in(w_in).astype(BF16)
    w_out_b = w_out.astype(BF16).reshape(depth, 4, GROUP_W, d)
    f1w1, f1w2 = ffn1_w1.astype(BF16), ffn1_w2.astype(BF16)
    f2w1, f2w2 = ffn2_w1.astype(BF16), ffn2_w2.astype(BF16)

    mix_names = dict(diff_lambda=diff_lambda, diff_subln_g=diff_subln_g, mla_q_norm_g=mla_q_norm_g,
                     mla_w_uq=mla_w_uq, mla_kv_norm_g=mla_kv_norm_g, mla_w_uk=mla_w_uk, mla_w_uv=mla_w_uv,
                     mla_out_g=mla_out_g, nsa_out_g=nsa_out_g, ssd_conv_w=ssd_conv_w, ssd_conv_b=ssd_conv_b,
                     ssd_dt_bias=ssd_dt_bias, ssd_a_log=ssd_a_log, ssd_d=ssd_d, ssd_norm_g=ssd_norm_g)

    x = jnp.concatenate([x_prompt.reshape(mp, d), x_sample.reshape(ms, d)], axis=0)
    pos_all = jnp.concatenate([jnp.tile(jnp.arange(tp), bp), jnp.tile(past_len + jnp.arange(ts), bs)])
    cos_t, sin_t = _rope_tables(pos_all)
    st_p, st_s = [], []
    for l in range(depth):
        p = {k: v[l] for k, v in mix_names.items()}
        lam_init = 0.8 - 0.6 * math.exp(-0.3 * l)
        wuq, wuk, wuv = _mla_weight_layout(mla_w_uq[l], mla_w_uk[l], mla_w_uv[l])
        x = _ffn(x, ffn1_pre_g[l], ffn1_post_g[l], f1w1[l], f1w2[l])
        u = _in_proj(x, mix_pre_g[l], w_in_p[l])
        q_mla, k_mla = _mla_prep(u, cos_t, sin_t, mla_q_norm_g[l], mla_kv_norm_g[l], wuq, wuk)
        oa_p = _diff_prompt(u, diff_lambda[l], diff_subln_g[l], b=bp, t=tp, lam_init=lam_init)
        ob_p = _mla_prompt(q_mla, k_mla, wuv, mla_out_g[l], b=bp, t=tp)
        oc_p = _nsa_prompt(u, nsa_out_g[l], b=bp, t=tp)
        u_p = u[:mp].reshape(bp, tp, -1)
        od_p, h_p = _ssd_prompt(u, p['ssd_conv_w'], p['ssd_conv_b'], p['ssd_dt_bias'], p['ssd_a_log'], p['ssd_d'],
                                p['ssd_norm_g'], b=bp, t=tp)
        conv_p = u_p[:, tp - (SSD_CONV - 1):, COL_SXBC:COL_SXBC + SSD_CONV_CH]
        new_p = (u_p[..., COL_DK:COL_DK + 256], k_mla[:mp, :160].reshape(bp, tp, 160),
                 u_p[..., COL_NKV:COL_NKV + 256], u_p[:, tp - w_buf:, COL_NKV + 256:COL_NKV + 384], conv_p, h_p)
        o_p = (oa_p, ob_p, oc_p, od_p)
        u_s = u[mp:].reshape(bs, ts, -1)
        pad_ts = lambda a: jnp.pad(a, ((0, 0), (0, TS_PAD - ts), (0, 0)))
        u_s8 = pad_ts(u_s)
        oa_s = _diff_decode(page_table, u_s8, cache_diff_kv, diff_lambda[l], diff_subln_g[l],
                            layer=l, ts=ts, lam_init=lam_init)[:, :ts]
        ob_s = _mla_decode(page_table, pad_ts(q_mla[mp:].reshape(bs, ts, -1)), pad_ts(k_mla[mp:].reshape(bs, ts, -1)),
                           cache_mla, wuv, mla_out_g[l], layer=l, ts=ts)[:, :ts]
        nkv_s = u_s[..., COL_NKV:COL_NKV + 384]
        win_seq = jnp.concatenate([cache_nsa_win[l], nkv_s[..., 256:]], axis=1)
        oc_s = _nsa_decode(page_table, u_s8, cache_nsa_kv, cache_nsa_win, nsa_out_g[l], layer=l, ts=ts)[:, :ts]
        od_s, h_s = _ssd_decode(u_s8, jnp.pad(state_ssd_conv[l], ((0, 0), (8 - (SSD_CONV - 1), 0), (0, 0))),
                                state_ssd_h.reshape(depth, bs, SSD_HEADS // 2, 128, 128), p['ssd_conv_w'],
                                p['ssd_conv_b'], p['ssd_dt_bias'], p['ssd_a_log'], p['ssd_d'], p['ssd_norm_g'],
                                layer=l, ts=ts)
        od_s = od_s[:, :ts]
        conv_s = jnp.concatenate([state_ssd_conv[l], u_s[..., COL_SXBC:COL_SXBC + SSD_CONV_CH]],
                                 axis=1)[:, -(SSD_CONV - 1):]
        new_s = (u_s[..., COL_DK:COL_DK + 256], k_mla[mp:, :160].reshape(bs, ts, 160), nkv_s[..., :256],
                 win_seq[:, -w_buf:], conv_s, h_s)
        o_s = (oa_s, ob_s, oc_s, od_s)
        o = [jnp.concatenate([a.reshape(mp, GROUP_W), b.reshape(ms, GROUP_W)], axis=0)
             for a, b in zip(o_p, o_s)]
        x = _out_proj(x, o[0], o[1], o[2], o[3], w_out_b[l], mix_post_g[l])
        x = _ffn(x, ffn2_pre_g[l], ffn2_post_g[l], f2w1[l], f2w2[l])
        st_p.append(new_p)
        st_s.append(new_s)

    def stacked(states, i):
        return jnp.stack([s[i] for s in states])

    return (x[:mp].reshape(bp, tp, d), x[mp:].reshape(bs, ts, d),
            stacked(st_p, 0), stacked(st_s, 0),
            stacked(st_p, 1), stacked(st_s, 1),
            stacked(st_p, 2), stacked(st_s, 2),
            stacked(st_p, 3), stacked(st_s, 3),
            stacked(st_p, 4), stacked(st_s, 4),
            stacked(st_p, 5), stacked(st_s, 5))
```
